```python
import math
import jax, jax.numpy as jnp
from jax import lax
import numpy as np

D_MODEL = 1024
BATCH = 16
SEQ = 256
DEPTH = 1
DEC_BATCH = 4
DEC_SEQ = 1024
PAST_LEN = 256

GRID_W = 64
HEAD_DIM = 64
N_HEADS = 8
N_KV_HEADS = 2
ATTN_W = N_HEADS * HEAD_DIM
KV_W = N_KV_HEADS * HEAD_DIM
D_CONV = D_MODEL // 2
MIX_W = ATTN_W + D_CONV
IN_W = ATTN_W + 2 * KV_W + 2 * D_CONV
CONV_WIDTH = 31
CONV_PAD = CONV_WIDTH // 2
Q_BLOCK = 128
ROPE_THETA = 10000.0
ROPE_AXIS_DIM = HEAD_DIM // 2
N_EXPERTS = 16
CAPACITY_FACTOR = 2
D_FF_EXPERT = 1024
EPS = 1e-6

kernel_name = "hybrid_dit_gqa_conformer_ecmoe_step"


def rms_norm(x, g):
    x32 = x.astype(jnp.float32)
    y = x32 * lax.rsqrt(jnp.mean(x32 * x32, axis=-1, keepdims=True) + EPS)
    return (y * g.astype(jnp.float32)).astype(x.dtype)


def layer_norm(x, g, b):
    x32 = x.astype(jnp.float32)
    mu = jnp.mean(x32, axis=-1, keepdims=True)
    var = jnp.mean(jnp.square(x32 - mu), axis=-1, keepdims=True)
    y = (x32 - mu) * lax.rsqrt(var + EPS)
    return (y * g.astype(jnp.float32) + b.astype(jnp.float32)).astype(x.dtype)


def modulation(cond, w_mod, b_mod):
    m = jax.nn.silu(cond) @ w_mod + b_mod
    return jnp.split(m[:, None, :], 6, axis=-1)


def rope_axis(x, pos):
    half = ROPE_AXIS_DIM // 2
    freqs = ROPE_THETA ** (-jnp.arange(half, dtype=jnp.float32) / half)
    ang = pos.astype(jnp.float32)[:, None] * freqs[None, :]
    cos = jnp.cos(ang)[None, :, None, :]
    sin = jnp.sin(ang)[None, :, None, :]
    x32 = x.astype(jnp.float32)
    x1, x2 = x32[..., :half], x32[..., half:]
    return jnp.concatenate([x1 * cos - x2 * sin, x2 * cos + x1 * sin], axis=-1).astype(x.dtype)


def axial_rope(x):
    n = x.shape[1]
    rows = n // GRID_W
    pos_row = jnp.repeat(jnp.arange(rows, dtype=jnp.int32), GRID_W)
    pos_col = jnp.tile(jnp.arange(GRID_W, dtype=jnp.int32), rows)
    return jnp.concatenate([rope_axis(x[..., :ROPE_AXIS_DIM], pos_row),
                            rope_axis(x[..., ROPE_AXIS_DIM:], pos_col)], axis=-1)


def block_attention(q, k, v):
    b, nq, h, d = q.shape
    nkv = k.shape[2]
    g = h // nkv
    nb = nq // Q_BLOCK
    qb = q.reshape(b, nb, Q_BLOCK, nkv, g, d).transpose(1, 0, 2, 3, 4, 5)
    scale = 1.0 / math.sqrt(d)

    def one_block(qblk):
        s = jnp.einsum('bqkgd,bskd->bkgqs', qblk, k, preferred_element_type=jnp.float32) * scale
        p = jax.nn.softmax(s, axis=-1).astype(v.dtype)
        return jnp.einsum('bkgqs,bskd->bqkgd', p, v)

    out = lax.map(one_block, qb)
    return out.transpose(1, 0, 2, 3, 4, 5).reshape(b, nq, h * d)


def conformer_conv(a, gt, conv_w, conv_b, ln_g, ln_b):
    u = a * jax.nn.sigmoid(gt)
    y = lax.conv_general_dilated(u, conv_w[:, None, :], window_strides=(1,),
                                 padding=[(CONV_PAD, CONV_PAD)],
                                 dimension_numbers=('NWC', 'WIO', 'NWC'),
                                 feature_group_count=D_CONV) + conv_b
    return jax.nn.silu(layer_norm(y, ln_g, ln_b))


def mixer(h, w_in, q_norm, k_norm, conv_w, conv_b, ln_g, ln_b, w_out, latent, k_ctx, v_ctx):
    b, n, _ = h.shape
    p = h @ w_in
    q = rms_norm(p[..., :ATTN_W].reshape(b, n, N_HEADS, HEAD_DIM), q_norm)
    k = rms_norm(p[..., ATTN_W:ATTN_W + KV_W].reshape(b, n, N_KV_HEADS, HEAD_DIM), k_norm)
    v = p[..., ATTN_W + KV_W:ATTN_W + 2 * KV_W].reshape(b, n, N_KV_HEADS, HEAD_DIM)
    a = p[..., ATTN_W + 2 * KV_W:ATTN_W + 2 * KV_W + D_CONV]
    gt = p[..., ATTN_W + 2 * KV_W + D_CONV:]
    if latent:
        q = axial_rope(q)
        k_all = jnp.concatenate([axial_rope(k), k_ctx.astype(k.dtype)], axis=1)
        v_all = jnp.concatenate([v, v_ctx.astype(v.dtype)], axis=1)
    else:
        k_all, v_all = k, v
    attn = block_attention(q, k_all, v_all)
    conv = conformer_conv(a, gt, conv_w, conv_b, ln_g, ln_b)
    out = jnp.concatenate([attn, conv], axis=-1) @ w_out
    return out, k, v


def ec_moe(h, w_router, w_gate, w_up, w_down):
    b, n, d = h.shape
    cap = CAPACITY_FACTOR * n // N_EXPERTS
    logits = jnp.einsum('bnd,de->bne', h, w_router, preferred_element_type=jnp.float32)
    aff = jax.nn.softmax(logits, axis=-1)
    gates, idx = lax.top_k(jnp.swapaxes(aff, 1, 2), cap)
    xs = jax.vmap(lambda hb, ib: hb[ib])(h, idx)
    hg = jnp.einsum('becd,edf->becf', xs, w_gate)
    hu = jnp.einsum('becd,edf->becf', xs, w_up)
    y = jnp.einsum('becf,efd->becd', jax.nn.silu(hg) * hu, w_down) * gates.astype(h.dtype)[..., None]
    return jax.vmap(lambda ib, yb: jnp.zeros((n, d), h.dtype).at[ib.reshape(-1)].add(yb.reshape(-1, d)))(idx, y)


def setup_inputs(seed: int = 0) -> dict:
    key = jax.random.key(seed)
    ks = jax.random.split(key, 24)
    f32 = jnp.float32
    nrm = lambda k, s, sc: jax.random.normal(k, s, f32) * sc
    gain = lambda k, s: 1.0 + 0.05 * jax.random.normal(k, s, f32)
    return {
        "x_prompt": nrm(ks[0], (BATCH, SEQ, D_MODEL), 1.0),
        "x_sample": nrm(ks[1], (DEC_BATCH, DEC_SEQ, D_MODEL), 1.0),
        "c": nrm(ks[2], (DEC_BATCH, D_MODEL), 1.0),
        "c_ctx": nrm(ks[3], (D_MODEL,), 1.0),
        "cache_k": nrm(ks[4], (DEC_BATCH, DEPTH, PAST_LEN, N_KV_HEADS, HEAD_DIM), 1.0),
        "cache_v": nrm(ks[5], (DEC_BATCH, DEPTH, PAST_LEN, N_KV_HEADS, HEAD_DIM), 1.0),
        "w_mod": nrm(ks[6], (DEPTH, D_MODEL, 6 * D_MODEL), 0.5 * D_MODEL ** -0.5),
        "b_mod": nrm(ks[7], (DEPTH, 6 * D_MODEL), 0.01),
        "g_pre_mix": gain(ks[8], (DEPTH, D_MODEL)),
        "g_post_mix": gain(ks[9], (DEPTH, D_MODEL)),
        "g_pre_ffn": gain(ks[10], (DEPTH, D_MODEL)),
        "g_post_ffn": gain(ks[11], (DEPTH, D_MODEL)),
        "w_in": nrm(ks[12], (DEPTH, D_MODEL, IN_W), D_MODEL ** -0.5),
        "q_norm": gain(ks[13], (DEPTH, HEAD_DIM)),
        "k_norm": gain(ks[14], (DEPTH, HEAD_DIM)),
        "conv_w": nrm(ks[15], (DEPTH, CONV_WIDTH, D_CONV), CONV_WIDTH ** -0.5),
        "conv_b": nrm(ks[16], (DEPTH, D_CONV), 0.01),
        "conv_ln_g": gain(ks[17], (DEPTH, D_CONV)),
        "conv_ln_b": nrm(ks[18], (DEPTH, D_CONV), 0.01),
        "w_out": nrm(ks[19], (DEPTH, MIX_W, D_MODEL), MIX_W ** -0.5),
        "w_router": nrm(ks[20], (DEPTH, D_MODEL, N_EXPERTS), D_MODEL ** -0.5),
        "w_gate": nrm(ks[21], (DEPTH, N_EXPERTS, D_MODEL, D_FF_EXPERT), D_MODEL ** -0.5),
        "w_up": nrm(ks[22], (DEPTH, N_EXPERTS, D_MODEL, D_FF_EXPERT), D_MODEL ** -0.5),
        "w_down": nrm(ks[23], (DEPTH, N_EXPERTS, D_FF_EXPERT, D_MODEL), D_FF_EXPERT ** -0.5),
    }


def reference(x_prompt, x_sample, c, c_ctx, cache_k, cache_v, w_mod, b_mod,
              g_pre_mix, g_post_mix, g_pre_ffn, g_post_ffn, w_in, q_norm, k_norm,
              conv_w, conv_b, conv_ln_g, conv_ln_b, w_out, w_router, w_gate, w_up, w_down):
    xp, xs = x_prompt, x_sample
    new_k, new_v = [], []
    for l in range(DEPTH):
        mix_args = (w_in[l], q_norm[l], k_norm[l], conv_w[l], conv_b[l],
                    conv_ln_g[l], conv_ln_b[l], w_out[l])
        moe_args = (w_router[l], w_gate[l], w_up[l], w_down[l])

        sh1, sc1, ga1, sh2, sc2, ga2 = modulation(c_ctx[None, :], w_mod[l], b_mod[l])
        h = rms_norm(xp, g_pre_mix[l]) * (1.0 + sc1) + sh1
        out, k_ctx, v_ctx = mixer(h, *mix_args, False, None, None)
        xp = xp + ga1 * rms_norm(out, g_post_mix[l])
        h = rms_norm(xp, g_pre_ffn[l]) * (1.0 + sc2) + sh2
        xp = xp + ga2 * rms_norm(ec_moe(h, *moe_args), g_post_ffn[l])
        new_k.append(k_ctx)
        new_v.append(v_ctx)

        sh1, sc1, ga1, sh2, sc2, ga2 = modulation(c, w_mod[l], b_mod[l])
        h = rms_norm(xs, g_pre_mix[l]) * (1.0 + sc1) + sh1
        out, _, _ = mixer(h, *mix_args, True, cache_k[:, l], cache_v[:, l])
        xs = xs + ga1 * rms_norm(out, g_post_mix[l])
        h = rms_norm(xs, g_pre_ffn[l]) * (1.0 + sc2) + sh2
        xs = xs + ga2 * rms_norm(ec_moe(h, *moe_args), g_post_ffn[l])

    k_state = jnp.stack(new_k, axis=1)
    v_state = jnp.stack(new_v, axis=1)
    return (xp, xs, k_state, v_state)
```

```python
import functools
import math

import jax
import jax.numpy as jnp
from jax import lax
from jax.experimental import pallas as pl
from jax.experimental.pallas import tpu as pltpu

F32 = jnp.float32
BF16 = jnp.bfloat16

D_MODEL = 1024
HEAD_DIM = 64
N_HEADS = 8
N_KV_HEADS = 2
ATTN_W = N_HEADS * HEAD_DIM
KV_W = N_KV_HEADS * HEAD_DIM
D_CONV = 512
IN_W = ATTN_W + 2 * KV_W + 2 * D_CONV
CONV_WIDTH = 31
CONV_PAD = CONV_WIDTH // 2
GRID_W = 64
ROPE_THETA = 10000.0
N_EXPERTS = 16
CAPACITY_FACTOR = 2
D_FF = 1024
EPS = 1e-6

LANES = 128
SUBLANES = 8
MXU_DIM = 256
VMEM_LIMIT = 56 * 1024 * 1024

TOKEN_TILE = 512
Q_TILE = 256
CONV_ROWS = 32
PAD_ROWS = 16


def _cparams(n_axes):
    return pltpu.CompilerParams(
        dimension_semantics=("arbitrary",) * n_axes, vmem_limit_bytes=VMEM_LIMIT)


def _sigmoid(x):
    return jax.nn.sigmoid(x)


def _nt_dot(a, b):
    return lax.dot_general(a, b, (((1,), (1,)), ((), ())), preferred_element_type=F32)


def _split_bf16(x):
    hi = x.astype(BF16)
    lo = (x - hi.astype(F32)).astype(BF16)
    return hi, lo


def _mod_kernel(cond_ref, w_ref, b_ref, o_ref):
    cnd = cond_ref[...]
    s = (cnd * _sigmoid(cnd)).astype(BF16)
    o_ref[...] = jnp.dot(s, w_ref[...].astype(BF16), preferred_element_type=F32) + b_ref[...]


def _modulation(cond8, w_mod, b_mod):
    n_out = w_mod.shape[1]
    blk = 1024
    return pl.pallas_call(
        _mod_kernel,
        grid=(n_out // blk,),
        in_specs=[pl.BlockSpec((8, D_MODEL), lambda j: (0, 0)),
                  pl.BlockSpec((D_MODEL, blk), lambda j: (0, j)),
                  pl.BlockSpec((1, blk), lambda j: (0, j))],
        out_specs=pl.BlockSpec((8, blk), lambda j: (0, j)),
        out_shape=jax.ShapeDtypeStruct((8, n_out), F32),
        compiler_params=_cparams(1),
        name="modulation",
    )(cond8, w_mod, b_mod.reshape(1, n_out))


def _head_norm(t, gain, bd_ref):
    w = t.shape[1]
    cw = min(w, MXU_DIM)
    t2 = t * t
    hi, lo = _split_bf16(t2)
    bd = bd_ref[:cw, :cw]
    parts = []
    for c in range(0, w, cw):
        parts.append(jnp.dot(hi[:, c:c + cw], bd, preferred_element_type=F32)
                     + jnp.dot(lo[:, c:c + cw], bd, preferred_element_type=F32))
    ss = parts[0] if len(parts) == 1 else jnp.concatenate(parts, axis=1)
    return t * lax.rsqrt(ss * (1.0 / HEAD_DIM) + EPS) * gain


def _rope(t, cos, sin_signed):
    rows = t.shape[0]
    lane = lax.broadcasted_iota(jnp.int32, (rows, LANES), 1)
    first_half = (lane & 31) < 16
    outs = []
    for c in range(0, t.shape[1], LANES):
        tc = t[:, c:c + LANES]
        fwd = pltpu.roll(tc, LANES - 16, 1)
        bwd = pltpu.roll(tc, 16, 1)
        partner = jnp.where(first_half, fwd, bwd)
        outs.append(tc * cos + partner * sin_signed)
    return outs[0] if len(outs) == 1 else jnp.concatenate(outs, axis=1)


def _mixin_kernel(*refs, latent):
    if latent:
        (x_ref, mod_ref, g_ref, w_ref, qg_ref, kg_ref, bd_ref, cos_ref, sin_ref,
         q_o, kd_o, vd_o, u_o) = refs
    else:
        (x_ref, mod_ref, g_ref, w_ref, qg_ref, kg_ref, bd_ref,
         q_o, kd_o, vd_o, u_o, k32_o, v32_o) = refs
    x = x_ref[...]
    m = mod_ref[0]
    sh1 = m[:, 0:D_MODEL]
    sc1 = m[:, D_MODEL:2 * D_MODEL]
    ms = jnp.mean(x * x, axis=-1, keepdims=True)
    h = (x * lax.rsqrt(ms + EPS)) * g_ref[...]
    h = h * (1.0 + sc1) + sh1
    p = jnp.dot(h.astype(BF16), w_ref[...], preferred_element_type=F32)
    q = _head_norm(p[:, :ATTN_W], qg_ref[...], bd_ref)
    k = _head_norm(p[:, ATTN_W:ATTN_W + KV_W], kg_ref[...], bd_ref)
    v = p[:, ATTN_W + KV_W:ATTN_W + 2 * KV_W]
    a = p[:, ATTN_W + 2 * KV_W:ATTN_W + 2 * KV_W + D_CONV]
    gt = p[:, ATTN_W + 2 * KV_W + D_CONV:]
    if latent:
        cos = cos_ref[...]
        sin = sin_ref[...]
        q = _rope(q, cos, sin)
        k_att = _rope(k, cos, sin)
    else:
        k32_o[...] = k
        v32_o[...] = v
        k_att = k
    q_o[...] = (q * (1.0 / math.sqrt(HEAD_DIM))).astype(BF16)
    rows = x.shape[0]
    low = lax.broadcasted_iota(jnp.int32, (rows, LANES), 1) < HEAD_DIM
    k_sw = pltpu.roll(k_att, HEAD_DIM, 1)
    kd_o[:, 0:LANES] = jnp.where(low, k_att, k_sw).astype(BF16)
    kd_o[:, LANES:2 * LANES] = jnp.where(low, k_sw, k_att).astype(BF16)
    v_sw = pltpu.roll(v, HEAD_DIM, 1)
    zero = jnp.zeros_like(v)
    vd_o[:, 0:LANES] = jnp.where(low, v, zero).astype(BF16)
    vd_o[:, LANES:2 * LANES] = jnp.where(low, zero, v_sw).astype(BF16)
    vd_o[:, 2 * LANES:3 * LANES] = jnp.where(low, v_sw, zero).astype(BF16)
    vd_o[:, 3 * LANES:4 * LANES] = jnp.where(low, zero, v).astype(BF16)
    u_o[...] = a * _sigmoid(gt)


def _mix_in(x2d, mod3, g_pre, w_in_bf, qg, kg, bd, rope_tabs, *, n_seq, mod_row0, latent):
    t = x2d.shape[0]
    tm = TOKEN_TILE
    tiles_per_seq = max(n_seq // tm, 1)
    if latent:
        mod_idx = lambda i: (mod_row0 + i // tiles_per_seq, 0, 0)
    else:
        mod_idx = lambda i: (mod_row0, 0, 0)
    in_specs = [pl.BlockSpec((tm, D_MODEL), lambda i: (i, 0)),
                pl.BlockSpec((1, 1, mod3.shape[2]), mod_idx),
                pl.BlockSpec((1, D_MODEL), lambda i: (0, 0)),
                pl.BlockSpec((D_MODEL, IN_W), lambda i: (0, 0)),
                pl.BlockSpec((1, ATTN_W), lambda i: (0, 0)),
                pl.BlockSpec((1, KV_W), lambda i: (0, 0)),
                pl.BlockSpec((MXU_DIM, MXU_DIM), lambda i: (0, 0))]
    args = [x2d, mod3, g_pre, w_in_bf, qg, kg, bd]
    out_shape = [jax.ShapeDtypeStruct((t, ATTN_W), BF16),
                 jax.ShapeDtypeStruct((t, 2 * LANES), BF16),
                 jax.ShapeDtypeStruct((t, 4 * LANES), BF16),
                 jax.ShapeDtypeStruct((t, D_CONV), F32)]
    out_specs = [pl.BlockSpec((tm, ATTN_W), lambda i: (i, 0)),
                 pl.BlockSpec((tm, 2 * LANES), lambda i: (i, 0)),
                 pl.BlockSpec((tm, 4 * LANES), lambda i: (i, 0)),
                 pl.BlockSpec((tm, D_CONV), lambda i: (i, 0))]
    if latent:
        cos, sin = rope_tabs
        in_specs += [pl.BlockSpec((tm, LANES), lambda i: (i % tiles_per_seq, 0)),
                     pl.BlockSpec((tm, LANES), lambda i: (i % tiles_per_seq, 0))]
        args += [cos, sin]
    else:
        out_shape += [jax.ShapeDtypeStruct((t, KV_W), F32), jax.ShapeDtypeStruct((t, KV_W), F32)]
        out_specs += [pl.BlockSpec((tm, KV_W), lambda i: (i, 0)),
                      pl.BlockSpec((tm, KV_W), lambda i: (i, 0))]
    return pl.pallas_call(
        functools.partial(_mixin_kernel, latent=latent),
        grid=(t // tm,),
        in_specs=in_specs, out_specs=out_specs, out_shape=out_shape,
        compiler_params=_cparams(1),
        name="mix_in_lat" if latent else "mix_in_ctx",
    )(*args)


def _attn_kernel(*refs, n_seq, latent):
    if latent:
        q_ref, kd_ref, vd_ref, ck_ref, cv_ref, o_ref = refs
    else:
        q_ref, kd_ref, vd_ref, o_ref = refs
    tq = min(Q_TILE, n_seq)
    low1 = lax.broadcasted_iota(jnp.int32, (1, LANES), 1) < HEAD_DIM
    if latent:
        n_ctx = ck_ref.shape[1]
        low_c = lax.broadcasted_iota(jnp.int32, (n_ctx, LANES), 1) < HEAD_DIM
        ck = ck_ref[0]
        ck_sw = pltpu.roll(ck, HEAD_DIM, 1)
        ckd = [jnp.where(low_c, ck, ck_sw).astype(BF16), jnp.where(low_c, ck_sw, ck).astype(BF16)]
        cv = cv_ref[0]
        cv_sw = pltpu.roll(cv, HEAD_DIM, 1)
        zc = jnp.zeros_like(cv)
        cvd = [jnp.where(low_c, cv, zc).astype(BF16), jnp.where(low_c, zc, cv_sw).astype(BF16),
               jnp.where(low_c, cv_sw, zc).astype(BF16), jnp.where(low_c, zc, cv).astype(BF16)]

    def body(i, carry):
        r0 = pl.multiple_of(i * tq, tq)
        for j in range(N_HEADS // 2):
            g = (2 * j) // (N_HEADS // N_KV_HEADS)
            qp = q_ref[pl.ds(r0, tq), j * LANES:(j + 1) * LANES]
            kg = kd_ref[:, g * LANES:(g + 1) * LANES]
            acc = jnp.zeros((tq, LANES), F32)
            for half in range(2):
                sel = low1 if half == 0 else jnp.logical_not(low1)
                qm = jnp.where(sel, qp, jnp.zeros_like(qp))
                s1 = _nt_dot(qm, kg)
                mx = jnp.max(s1, axis=-1, keepdims=True)
                if latent:
                    s2 = _nt_dot(qm, ckd[g])
                    mx = jnp.maximum(mx, jnp.max(s2, axis=-1, keepdims=True))
                p1 = jnp.exp(s1 - mx)
                den = jnp.sum(p1, axis=-1, keepdims=True)
                vv = vd_ref[:, (2 * g + half) * LANES:(2 * g + half + 1) * LANES]
                o = jnp.dot(p1.astype(BF16), vv, preferred_element_type=F32)
                if latent:
                    p2 = jnp.exp(s2 - mx)
                    den = den + jnp.sum(p2, axis=-1, keepdims=True)
                    o = o + jnp.dot(p2.astype(BF16), cvd[2 * g + half], preferred_element_type=F32)
                acc = acc + o * (1.0 / den)
            o_ref[pl.ds(r0, tq), j * LANES:(j + 1) * LANES] = acc.astype(BF16)
        return carry

    lax.fori_loop(0, n_seq // tq, body, 0)


def _attention(q, kd, vd, cache, *, n_seq, latent):
    t = q.shape[0]
    in_specs = [pl.BlockSpec((n_seq, ATTN_W), lambda b: (b, 0)),
                pl.BlockSpec((n_seq, 2 * LANES), lambda b: (b, 0)),
                pl.BlockSpec((n_seq, 4 * LANES), lambda b: (b, 0))]
    args = [q, kd, vd]
    if latent:
        ck, cv = cache
        in_specs += [pl.BlockSpec((1,) + ck.shape[1:], lambda b: (b, 0, 0)),
                     pl.BlockSpec((1,) + cv.shape[1:], lambda b: (b, 0, 0))]
        args += [ck, cv]
    return pl.pallas_call(
        functools.partial(_attn_kernel, n_seq=n_seq, latent=latent),
        grid=(t // n_seq,),
        in_specs=in_specs,
        out_specs=pl.BlockSpec((n_seq, ATTN_W), lambda b: (b, 0)),
        out_shape=jax.ShapeDtypeStruct((t, ATTN_W), BF16),
        compiler_params=_cparams(1),
        name="attn_lat" if latent else "attn_ctx",
    )(*args)


def _conv_kernel(u_ref, w_ref, cb_ref, lg_ref, lb_ref, o_ref, pad_ref, *, n_seq):
    zeros = jnp.zeros((PAD_ROWS, D_CONV), F32)
    pad_ref[0:PAD_ROWS, :] = zeros
    pad_ref[PAD_ROWS + n_seq:2 * PAD_ROWS + n_seq, :] = zeros
    pad_ref[PAD_ROWS:PAD_ROWS + n_seq, :] = u_ref[...]
    base = PAD_ROWS - CONV_PAD
    for c in range(n_seq // CONV_ROWS):
        r0 = c * CONV_ROWS
        acc = jnp.zeros((CONV_ROWS, D_CONV), F32) + cb_ref[...]
        for k in range(CONV_WIDTH):
            s = r0 + k + base
            acc = acc + pad_ref[s:s + CONV_ROWS, :] * w_ref[k:k + 1, :]
        mu = jnp.mean(acc, axis=-1, keepdims=True)
        cen = acc - mu
        var = jnp.mean(cen * cen, axis=-1, keepdims=True)
        y = cen * lax.rsqrt(var + EPS) * lg_ref[...] + lb_ref[...]
        o_ref[r0:r0 + CONV_ROWS, :] = (y * _sigmoid(y)).astype(BF16)


def _conv(u, conv_w, conv_b, ln_g, ln_b, *, n_seq, name):
    t = u.shape[0]
    vec = pl.BlockSpec((1, D_CONV), lambda b: (0, 0))
    return pl.pallas_call(
        functools.partial(_conv_kernel, n_seq=n_seq),
        grid=(t // n_seq,),
        in_specs=[pl.BlockSpec((n_seq, D_CONV), lambda b: (b, 0)),
                  pl.BlockSpec((CONV_WIDTH, D_CONV), lambda b: (0, 0)), vec, vec, vec],
        out_specs=pl.BlockSpec((n_seq, D_CONV), lambda b: (b, 0)),
        out_shape=jax.ShapeDtypeStruct((t, D_CONV), BF16),
        scratch_shapes=[pltpu.VMEM((n_seq + 2 * PAD_ROWS, D_CONV), F32)],
        compiler_params=_cparams(1),
        name=name,
    )(u, conv_w, conv_b, ln_g, ln_b)


def _mixout_kernel(attn_ref, conv_ref, x_ref, mod_ref, wo_ref, gpost_ref, gpre_ref, wr_ref,
                   x1_o, h2_o, aff_o, *, n_seq):
    tm = x_ref.shape[0]
    m = mod_ref[0]
    ga1 = m[:, 2 * D_MODEL:3 * D_MODEL]
    sh2 = m[:, 3 * D_MODEL:4 * D_MODEL]
    sc2 = m[:, 4 * D_MODEL:5 * D_MODEL]
    out = (jnp.dot(attn_ref[...], wo_ref[0:ATTN_W, :], preferred_element_type=F32)
           + jnp.dot(conv_ref[...], wo_ref[ATTN_W:, :], preferred_element_type=F32))
    ms = jnp.mean(out * out, axis=-1, keepdims=True)
    x1 = x_ref[...] + ga1 * ((out * lax.rsqrt(ms + EPS)) * gpost_ref[...])
    x1_o[...] = x1
    ms2 = jnp.mean(x1 * x1, axis=-1, keepdims=True)
    h2 = ((x1 * lax.rsqrt(ms2 + EPS)) * gpre_ref[...]) * (1.0 + sc2) + sh2
    h2_o[...] = h2.astype(BF16)
    h_hi, h_lo = _split_bf16(h2)
    w_hi, w_lo = _split_bf16(wr_ref[...])
    logits = (jnp.dot(h_hi, w_hi, preferred_element_type=F32)
              + jnp.dot(h_hi, w_lo, preferred_element_type=F32)
              + jnp.dot(h_lo, w_hi, preferred_element_type=F32))
    lt = logits.T[0:N_EXPERTS, :]
    mx = jnp.max(lt, axis=0, keepdims=True)
    e = jnp.exp(lt - mx)
    aff = e / jnp.sum(e, axis=0, keepdims=True)
    seqs = max(tm // n_seq, 1)
    w = tm // seqs
    for s in range(seqs):
        aff_o[s] = aff[:, s * w:(s + 1) * w]


def _mix_out(attn, conv, x2d, mod3, wo_bf, g_post, g_pre_ffn, wr_pad, *, n_seq, mod_row0, latent):
    t = x2d.shape[0]
    tm = TOKEN_TILE
    n_b = t // n_seq
    tiles_per_seq = max(n_seq // tm, 1)
    seqs_per_tile = max(tm // n_seq, 1)
    if latent:
        mod_idx = lambda i: (mod_row0 + i // tiles_per_seq, 0, 0)
    else:
        mod_idx = lambda i: (mod_row0, 0, 0)
    aff_w = min(tm, n_seq)
    vec = pl.BlockSpec((1, D_MODEL), lambda i: (0, 0))
    return pl.pallas_call(
        functools.partial(_mixout_kernel, n_seq=n_seq),
        grid=(t // tm,),
        in_specs=[pl.BlockSpec((tm, ATTN_W), lambda i: (i, 0)),
                  pl.BlockSpec((tm, D_CONV), lambda i: (i, 0)),
                  pl.BlockSpec((tm, D_MODEL), lambda i: (i, 0)),
                  pl.BlockSpec((1, 1, mod3.shape[2]), mod_idx),
                  pl.BlockSpec((D_MODEL, D_MODEL), lambda i: (0, 0)),
                  vec, vec,
                  pl.BlockSpec((D_MODEL, LANES), lambda i: (0, 0))],
        out_specs=[pl.BlockSpec((tm, D_MODEL), lambda i: (i, 0)),
                   pl.BlockSpec((tm, D_MODEL), lambda i: (i, 0)),
                   pl.BlockSpec((seqs_per_tile, N_EXPERTS, aff_w),
                                lambda i: (i // tiles_per_seq, 0, i % tiles_per_seq))],
        out_shape=[jax.ShapeDtypeStruct((t, D_MODEL), F32),
                   jax.ShapeDtypeStruct((t, D_MODEL), BF16),
                   jax.ShapeDtypeStruct((n_b, N_EXPERTS, n_seq), F32)],
        compiler_params=_cparams(1),
        name="mix_out_lat" if latent else "mix_out_ctx",
    )(attn, conv, x2d, mod3, wo_bf, g_post, g_pre_ffn, wr_pad)


def _route_kernel(aff_ref, slot_o, slot_t_o, *, n_seq, cap):
    n_b = aff_ref.shape[0]
    rows = n_b * N_EXPERTS
    a = aff_ref[...].reshape(rows, n_seq)
    capf = float(cap)

    def count_ge(thr_bits):
        thr = lax.bitcast_convert_type(thr_bits, F32)
        return jnp.sum(jnp.where(a >= thr, 1.0, 0.0), axis=-1, keepdims=True)

    def bis(i, lo):
        cand = lo | jnp.left_shift(jnp.int32(1), 29 - i)
        return jnp.where(count_ge(cand) >= capf, cand, lo)

    lo = lax.fori_loop(0, 30, bis, jnp.zeros((rows, 1), jnp.int32))
    kth = lax.bitcast_convert_type(lo, F32)
    above = lax.bitcast_convert_type(lo + 1, F32)
    gt = a >= above
    eq = jnp.logical_and(a >= kth, jnp.logical_not(gt))
    need = capf - jnp.sum(jnp.where(gt, 1.0, 0.0), axis=-1, keepdims=True)
    ri = lax.broadcasted_iota(jnp.int32, (n_seq, n_seq), 0)
    ci = lax.broadcasted_iota(jnp.int32, (n_seq, n_seq), 1)
    upper = jnp.where(ri <= ci, 1.0, 0.0).astype(BF16)
    eq_f = jnp.where(eq, 1.0, 0.0)
    eq_rank = jnp.dot(eq_f.astype(BF16), upper, preferred_element_type=F32)
    sel_f = jnp.where(gt, 1.0, jnp.where(eq_rank <= need, eq_f, 0.0))
    pos = jnp.dot(sel_f.astype(BF16), upper, preferred_element_type=F32) - 1.0
    slot = jnp.where(sel_f > 0.5, pos, -1.0)
    slot_o[...] = slot.reshape(n_b, N_EXPERTS, n_seq)
    fill = jnp.full((LANES - N_EXPERTS, n_seq), -1.0, F32)
    for b in range(n_b):
        sb = jnp.concatenate([slot[b * N_EXPERTS:(b + 1) * N_EXPERTS, :], fill], axis=0)
        slot_t_o[b] = sb.T


def _route(aff, *, n_seq, cap, name):
    n_b = aff.shape[0]
    return pl.pallas_call(
        functools.partial(_route_kernel, n_seq=n_seq, cap=cap),
        out_shape=[jax.ShapeDtypeStruct((n_b, N_EXPERTS, n_seq), F32),
                   jax.ShapeDtypeStruct((n_b, n_seq, LANES), F32)],
        compiler_params=pltpu.CompilerParams(vmem_limit_bytes=VMEM_LIMIT),
        name=name,
    )(aff)


def _gather_kernel(slot_ref, aff_ref, h_ref, xs_o, gate_o, p_ref, *, n_seq, cap):
    slot = slot_ref[0]
    aff = aff_ref[0]
    srow = lax.broadcasted_iota(jnp.int32, (cap, n_seq), 0).astype(F32)
    for e in range(N_EXPERTS):
        hit = jnp.broadcast_to(slot[e:e + 1, :], (cap, n_seq)) == srow
        p_ref[e * cap:(e + 1) * cap, :] = jnp.where(hit, 1.0, 0.0).astype(BF16)
        ga = jnp.where(hit, jnp.broadcast_to(aff[e:e + 1, :], (cap, n_seq)), 0.0)
        gate_o[e] = jnp.sum(ga, axis=-1, keepdims=True)
    xs = jnp.dot(p_ref[...], h_ref[...], preferred_element_type=F32).astype(BF16)
    for e in range(N_EXPERTS):
        xs_o[e] = xs[e * cap:(e + 1) * cap, :]


def _gather(slot, aff, h2, *, n_seq, cap, name):
    n_b = slot.shape[0]
    return pl.pallas_call(
        functools.partial(_gather_kernel, n_seq=n_seq, cap=cap),
        grid=(n_b,),
        in_specs=[pl.BlockSpec((1, N_EXPERTS, n_seq), lambda b: (b, 0, 0)),
                  pl.BlockSpec((1, N_EXPERTS, n_seq), lambda b: (b, 0, 0)),
                  pl.BlockSpec((n_seq, D_MODEL), lambda b: (b, 0))],
        out_specs=[pl.BlockSpec((N_EXPERTS, cap, D_MODEL), lambda b: (0, b, 0)),
                   pl.BlockSpec((N_EXPERTS, cap, 1), lambda b: (0, b, 0))],
        out_shape=[jax.ShapeDtypeStruct((N_EXPERTS, n_b * cap, D_MODEL), BF16),
                   jax.ShapeDtypeStruct((N_EXPERTS, n_b * cap, 1), F32)],
        scratch_shapes=[pltpu.VMEM((N_EXPERTS * cap, n_seq), BF16)],
        compiler_params=_cparams(1),
        name=name,
    )(slot, aff, h2)


FF_CHUNK = 256


def _ffn_kernel(xa_ref, xb_ref, ga_ref, gb_ref, wg_ref, wu_ref, wd_ref, ya_o, yb_o):
    for x_ref, g_ref, y_o in ((xa_ref, ga_ref, ya_o), (xb_ref, gb_ref, yb_o)):
        x = x_ref[0]
        y = jnp.zeros((x.shape[0], D_MODEL), F32)
        for c in range(0, D_FF, FF_CHUNK):
            wg = wg_ref[0, 0, :, c:c + FF_CHUNK].astype(BF16)
            wu = wu_ref[0, 0, :, c:c + FF_CHUNK].astype(BF16)
            wd = wd_ref[0, 0, c:c + FF_CHUNK, :].astype(BF16)
            hg = jnp.dot(x, wg, preferred_element_type=F32)
            hu = jnp.dot(x, wu, preferred_element_type=F32)
            act = ((hg * _sigmoid(hg)) * hu).astype(BF16)
            y = y + jnp.dot(act, wd, preferred_element_type=F32)
        y_o[0] = (y * g_ref[0]).astype(BF16)


def _ffn(xs_a, xs_b, gate_a, gate_b, w_gate, w_up, w_down):
    ra, rb = xs_a.shape[1], xs_b.shape[1]
    wspec = lambda r, c: pl.BlockSpec((1, 1, r, c), lambda e: (0, e, 0, 0))
    return pl.pallas_call(
        _ffn_kernel,
        grid=(N_EXPERTS,),
        in_specs=[pl.BlockSpec((1, ra, D_MODEL), lambda e: (e, 0, 0)),
                  pl.BlockSpec((1, rb, D_MODEL), lambda e: (e, 0, 0)),
                  pl.BlockSpec((1, ra, 1), lambda e: (e, 0, 0)),
                  pl.BlockSpec((1, rb, 1), lambda e: (e, 0, 0)),
                  wspec(D_MODEL, D_FF), wspec(D_MODEL, D_FF), wspec(D_FF, D_MODEL)],
        out_specs=[pl.BlockSpec((1, ra, D_MODEL), lambda e: (e, 0, 0)),
                   pl.BlockSpec((1, rb, D_MODEL), lambda e: (e, 0, 0))],
        out_shape=[jax.ShapeDtypeStruct((N_EXPERTS, ra, D_MODEL), BF16),
                   jax.ShapeDtypeStruct((N_EXPERTS, rb, D_MODEL), BF16)],
        compiler_params=_cparams(1),
        name="expert_ffn",
    )(xs_a, xs_b, gate_a, gate_b, w_gate, w_up, w_down)


def _combine_kernel(slot_t_ref, ys_ref, x1_ref, mod_ref, g_ref, o_ref, *, n_seq, cap):
    n_slots = N_EXPERTS * cap
    st = slot_t_ref[0].astype(BF16)
    er = lax.broadcasted_iota(jnp.int32, (LANES, n_slots), 0)
    ec = lax.broadcasted_iota(jnp.int32, (LANES, n_slots), 1)
    expand = jnp.where(er == (ec >> (cap.bit_length() - 1)), 1.0, 0.0).astype(BF16)
    slot_x = jnp.dot(st, expand, preferred_element_type=F32)
    col_slot = (lax.broadcasted_iota(jnp.int32, (n_seq, n_slots), 1) & (cap - 1)).astype(F32)
    pt = jnp.where(slot_x == col_slot, 1.0, 0.0).astype(BF16)
    moe = jnp.dot(pt, ys_ref[...].reshape(n_slots, D_MODEL), preferred_element_type=F32)
    ga2 = mod_ref[0][:, 5 * D_MODEL:6 * D_MODEL]
    ms = jnp.mean(moe * moe, axis=-1, keepdims=True)
    o_ref[...] = x1_ref[...] + ga2 * ((moe * lax.rsqrt(ms + EPS)) * g_ref[...])


def _combine(slot_t, ys, x1, mod3, g_post_ffn, *, n_seq, cap, mod_row0, latent, name):
    n_b = slot_t.shape[0]
    if latent:
        mod_idx = lambda b: (mod_row0 + b, 0, 0)
    else:
        mod_idx = lambda b: (mod_row0, 0, 0)
    return pl.pallas_call(
        functools.partial(_combine_kernel, n_seq=n_seq, cap=cap),
        grid=(n_b,),
        in_specs=[pl.BlockSpec((1, n_seq, LANES), lambda b: (b, 0, 0)),
                  pl.BlockSpec((N_EXPERTS, cap, D_MODEL), lambda b: (0, b, 0)),
                  pl.BlockSpec((n_seq, D_MODEL), lambda b: (b, 0)),
                  pl.BlockSpec((1, 1, mod3.shape[2]), mod_idx),
                  pl.BlockSpec((1, D_MODEL), lambda b: (0, 0))],
        out_specs=pl.BlockSpec((n_seq, D_MODEL), lambda b: (b, 0)),
        out_shape=jax.ShapeDtypeStruct((n_b * n_seq, D_MODEL), F32),
        compiler_params=_cparams(1),
        name=name,
    )(slot_t, ys, x1, mod3, g_post_ffn)


def _rope_tables(n_seq):
    half = HEAD_DIM // 4
    freqs = ROPE_THETA ** (-jnp.arange(half, dtype=F32) / half)
    lane = jnp.arange(LANES)
    d = lane % HEAD_DIM
    j = d % half
    tok = jnp.arange(n_seq, dtype=jnp.int32)
    pos_row = (tok // GRID_W).astype(F32)
    pos_col = (tok % GRID_W).astype(F32)
    pos = jnp.where((d < HEAD_DIM // 2)[None, :], pos_row[:, None], pos_col[:, None])
    ang = pos * freqs[j][None, :]
    sign = jnp.where((d % (2 * half)) < half, -1.0, 1.0).astype(F32)
    return jnp.cos(ang), jnp.sin(ang) * sign[None, :]


def _path(x, mod3, mod_row0, latent, cache, wts):
    (g_pre_mix, g_post_mix, g_pre_ffn, g_post_ffn, w_in_bf, qg, kg, bd, conv_w, conv_b, ln_g, ln_b,
     wo_bf, wr_pad) = wts
    n_b, n_seq, _ = x.shape
    x2d = x.reshape(n_b * n_seq, D_MODEL)
    rope_tabs = _rope_tables(n_seq) if latent else None
    outs = _mix_in(x2d, mod3, g_pre_mix, w_in_bf, qg, kg, bd, rope_tabs,
                   n_seq=n_seq, mod_row0=mod_row0, latent=latent)
    q, kd, vd, u = outs[:4]
    tag = "lat" if latent else "ctx"
    attn = _attention(q, kd, vd, cache, n_seq=n_seq, latent=latent)
    conv = _conv(u, conv_w, conv_b, ln_g, ln_b, n_seq=n_seq, name="conv_" + tag)
    x1, h2, aff = _mix_out(attn, conv, x2d, mod3, wo_bf, g_post_mix, g_pre_ffn, wr_pad,
                           n_seq=n_seq, mod_row0=mod_row0, latent=latent)
    cap = CAPACITY_FACTOR * n_seq // N_EXPERTS
    slot, slot_t = _route(aff, n_seq=n_seq, cap=cap, name="route_" + tag)
    xs, gate = _gather(slot, aff, h2, n_seq=n_seq, cap=cap, name="gather_" + tag)
    return dict(x1=x1, slot_t=slot_t, xs=xs, gate=gate, cap=cap, n_seq=n_seq, n_b=n_b,
                kv=outs[4:], mod_row0=mod_row0, latent=latent, tag=tag)


def kernel(x_prompt, x_sample, c, c_ctx, cache_k, cache_v, w_mod, b_mod, g_pre_mix, g_post_mix,
           g_pre_ffn, g_post_ffn, w_in, q_norm, k_norm, conv_w, conv_b, conv_ln_g, conv_ln_b,
           w_out, w_router, w_gate, w_up, w_down):
    l = 0
    n_lat = x_sample.shape[0]
    cond8 = jnp.concatenate([c_ctx[None, :], c, jnp.zeros((8 - 1 - n_lat, D_MODEL), F32)], axis=0)
    mod = _modulation(cond8, w_mod[l], b_mod[l])
    mod3 = mod.reshape(8, 1, mod.shape[1])

    bd_r = jnp.arange(MXU_DIM) // HEAD_DIM
    bd = (bd_r[:, None] == bd_r[None, :]).astype(BF16)
    row = lambda v: v.reshape(1, -1)
    wr_pad = jnp.pad(w_router[l], ((0, 0), (0, LANES - N_EXPERTS)))
    wts = (row(g_pre_mix[l]), row(g_post_mix[l]), row(g_pre_ffn[l]), row(g_post_ffn[l]),
           w_in[l].astype(BF16), row(jnp.tile(q_norm[l], N_HEADS)), row(jnp.tile(k_norm[l], N_KV_HEADS)),
           bd, conv_w[l], row(conv_b[l]), row(conv_ln_g[l]), row(conv_ln_b[l]),
           w_out[l].astype(BF16), wr_pad)

    past = cache_k.shape[2]
    cache = (cache_k[:, l].reshape(n_lat, past, KV_W), cache_v[:, l].reshape(n_lat, past, KV_W))
    ctx = _path(x_prompt, mod3, 0, False, None, wts)
    lat = _path(x_sample, mod3, 1, True, cache, wts)

    ys_ctx, ys_lat = _ffn(ctx["xs"], lat["xs"], ctx["gate"], lat["gate"], w_gate, w_up, w_down)

    outs = []
    for p, ys, x in ((ctx, ys_ctx, x_prompt), (lat, ys_lat, x_sample)):
        y = _combine(p["slot_t"], ys, p["x1"], mod3, row(g_post_ffn[l]), n_seq=p["n_seq"], cap=p["cap"],
                     mod_row0=p["mod_row0"], latent=p["latent"], name="combine_" + p["tag"])
        outs.append(y.reshape(x.shape))
    k32, v32 = ctx["kv"]
    b, s = x_prompt.shape[0], x_prompt.shape[1]
    new_k = k32.reshape(b, 1, s, N_KV_HEADS, HEAD_DIM)
    new_v = v32.reshape(b, 1, s, N_KV_HEADS, HEAD_DIM)
    return (outs[0], outs[1], new_k, new_v)
```

```python
import functools
import math

import jax
import jax.numpy as jnp
import numpy as np
from jax import lax
from jax.experimental import pallas as pl
from jax.experimental.pallas import tpu as pltpu

F32 = jnp.float32
BF16 = jnp.bfloat16

D_MODEL = 1024
HEAD_DIM = 64
N_HEADS = 8
N_KV_HEADS = 2
ATTN_W = N_HEADS * HEAD_DIM
KV_W = N_KV_HEADS * HEAD_DIM
D_CONV = 512
IN_W = ATTN_W + 2 * KV_W + 2 * D_CONV
CONV_WIDTH = 31
CONV_PAD = CONV_WIDTH // 2
GRID_W = 64
ROPE_THETA = 10000.0
N_EXPERTS = 16
CAPACITY_FACTOR = 2
D_FF = 1024
EPS = 1e-6
LOG2_E = 1.4426950408889634

LANES = 128
SUBLANES = 8
MXU_DIM = 256
VMEM_LIMIT = 56 * 1024 * 1024

TOKEN_TILE = 512
Q_TILE = 256
MIXOUT_SUB = 256
CONV_ROWS = 128
LN_ROWS = 64
PAD_ROWS = 16


def _cparams(n_axes):
    return pltpu.CompilerParams(
        dimension_semantics=("arbitrary",) * n_axes, vmem_limit_bytes=VMEM_LIMIT)


def _sigmoid(x):
    return jax.nn.sigmoid(x)


def _nt_dot(a, b):
    return lax.dot_general(a, b, (((1,), (1,)), ((), ())), preferred_element_type=F32)


def _split_bf16(x):
    hi = x.astype(BF16)
    lo = (x - hi.astype(F32)).astype(BF16)
    return hi, lo


def _mod_kernel(cond_ref, w_ref, b_ref, o_ref):
    cnd = cond_ref[...]
    s = (cnd * _sigmoid(cnd)).astype(BF16)
    o_ref[...] = jnp.dot(s, w_ref[...].astype(BF16), preferred_element_type=F32) + b_ref[...]


def _modulation(cond8, w_mod, b_mod):
    n_out = w_mod.shape[1]
    blk = 1024
    return pl.pallas_call(
        _mod_kernel,
        grid=(n_out // blk,),
        in_specs=[pl.BlockSpec((8, D_MODEL), lambda j: (0, 0)),
                  pl.BlockSpec((D_MODEL, blk), lambda j: (0, j)),
                  pl.BlockSpec((1, blk), lambda j: (0, j))],
        out_specs=pl.BlockSpec((8, blk), lambda j: (0, j)),
        out_shape=jax.ShapeDtypeStruct((8, n_out), F32),
        compiler_params=_cparams(1),
        name="modulation",
    )(cond8, w_mod, b_mod.reshape(1, n_out))


def _head_norm(t, gain, bd_ref):
    w = t.shape[1]
    cw = min(w, MXU_DIM)
    t2 = t * t
    hi, lo = _split_bf16(t2)
    bd = bd_ref[:cw, :cw]
    parts = []
    for c in range(0, w, cw):
        parts.append(jnp.dot(hi[:, c:c + cw], bd, preferred_element_type=F32)
                     + jnp.dot(lo[:, c:c + cw], bd, preferred_element_type=F32))
    ss = parts[0] if len(parts) == 1 else jnp.concatenate(parts, axis=1)
    return t * lax.rsqrt(ss * (1.0 / HEAD_DIM) + EPS) * gain


def _rope(t, cos, sin_signed):
    rows = t.shape[0]
    lane = lax.broadcasted_iota(jnp.int32, (rows, LANES), 1)
    first_half = (lane & 31) < 16
    outs = []
    for c in range(0, t.shape[1], LANES):
        tc = t[:, c:c + LANES]
        fwd = pltpu.roll(tc, LANES - 16, 1)
        bwd = pltpu.roll(tc, 16, 1)
        partner = jnp.where(first_half, fwd, bwd)
        outs.append(tc * cos + partner * sin_signed)
    return outs[0] if len(outs) == 1 else jnp.concatenate(outs, axis=1)


def _attn_layouts(k, k_sw, v, v_sw):
    lane = lax.broadcasted_iota(jnp.int32, k.shape, 1)
    low = lane < HEAD_DIM
    one_hi = jnp.where(lane == HEAD_DIM, 1.0, 0.0)
    one_lo = jnp.where(lane == 0, 1.0, 0.0)
    kd = [jnp.where(low, k, k_sw).astype(BF16), jnp.where(low, k_sw, k).astype(BF16)]
    vd = [jnp.where(low, v, one_hi).astype(BF16), jnp.where(low, one_lo, v_sw).astype(BF16),
          jnp.where(low, v_sw, one_hi).astype(BF16), jnp.where(low, one_lo, v).astype(BF16)]
    return kd, vd


def _mixin_kernel(*refs, latent):
    if latent:
        (x_ref, mod_ref, g_ref, w_ref, qg_ref, kg_ref, bd_ref, cos_ref, sin_ref,
         q_o, kd_o, vd_o, u_o, w_bf) = refs
    else:
        (x_ref, mod_ref, g_ref, w_ref, qg_ref, kg_ref, bd_ref,
         q_o, kd_o, vd_o, u_o, k32_o, v32_o, w_bf) = refs

    @pl.when(pl.program_id(0) == 0)
    def _():
        w_bf[...] = w_ref[...].astype(BF16)

    x = x_ref[...]
    m = mod_ref[0]
    sh1 = m[:, 0:D_MODEL]
    sc1 = m[:, D_MODEL:2 * D_MODEL]
    ms = jnp.mean(x * x, axis=-1, keepdims=True)
    h = (x * lax.rsqrt(ms + EPS)) * g_ref[...]
    h = h * (1.0 + sc1) + sh1
    p = jnp.dot(h.astype(BF16), w_bf[...], preferred_element_type=F32)
    q = _head_norm(p[:, :ATTN_W], qg_ref[...], bd_ref)
    k = _head_norm(p[:, ATTN_W:ATTN_W + KV_W], kg_ref[...], bd_ref)
    v = p[:, ATTN_W + KV_W:ATTN_W + 2 * KV_W]
    a = p[:, ATTN_W + 2 * KV_W:ATTN_W + 2 * KV_W + D_CONV]
    gt = p[:, ATTN_W + 2 * KV_W + D_CONV:]
    if latent:
        cos = cos_ref[...]
        sin = sin_ref[...]
        q = _rope(q, cos, sin)
        k_att = _rope(k, cos, sin)
    else:
        k32_o[...] = k
        v32_o[...] = v
        k_att = k
    q_o[...] = (q * (LOG2_E / math.sqrt(HEAD_DIM))).astype(BF16)
    k_sw = pltpu.roll(k_att, HEAD_DIM, 1)
    v_sw = pltpu.roll(v, HEAD_DIM, 1)
    kd, vd = _attn_layouts(k_att, k_sw, v, v_sw)
    for i, t in enumerate(kd):
        kd_o[:, i * LANES:(i + 1) * LANES] = t
    for i, t in enumerate(vd):
        vd_o[:, i * LANES:(i + 1) * LANES] = t
    u_o[...] = a * _sigmoid(gt)


def _mix_in(x2d, mod3, g_pre, w_in_bf, qg, kg, bd, rope_tabs, *, n_seq, mod_row0, latent):
    t = x2d.shape[0]
    tm = TOKEN_TILE
    tiles_per_seq = max(n_seq // tm, 1)
    if latent:
        mod_idx = lambda i: (mod_row0 + i // tiles_per_seq, 0, 0)
    else:
        mod_idx = lambda i: (mod_row0, 0, 0)
    in_specs = [pl.BlockSpec((tm, D_MODEL), lambda i: (i, 0)),
                pl.BlockSpec((1, 1, mod3.shape[2]), mod_idx),
                pl.BlockSpec((1, D_MODEL), lambda i: (0, 0)),
                pl.BlockSpec((D_MODEL, IN_W), lambda i: (0, 0)),
                pl.BlockSpec((1, ATTN_W), lambda i: (0, 0)),
                pl.BlockSpec((1, KV_W), lambda i: (0, 0)),
                pl.BlockSpec((MXU_DIM, MXU_DIM), lambda i: (0, 0))]
    args = [x2d, mod3, g_pre, w_in_bf, qg, kg, bd]
    out_shape = [jax.ShapeDtypeStruct((t, ATTN_W), BF16),
                 jax.ShapeDtypeStruct((t, 2 * LANES), BF16),
                 jax.ShapeDtypeStruct((t, 4 * LANES), BF16),
                 jax.ShapeDtypeStruct((t, D_CONV), F32)]
    out_specs = [pl.BlockSpec((tm, ATTN_W), lambda i: (i, 0)),
                 pl.BlockSpec((tm, 2 * LANES), lambda i: (i, 0)),
                 pl.BlockSpec((tm, 4 * LANES), lambda i: (i, 0)),
                 pl.BlockSpec((tm, D_CONV), lambda i: (i, 0))]
    if latent:
        cos, sin = rope_tabs
        in_specs += [pl.BlockSpec((tm, LANES), lambda i: (i % tiles_per_seq, 0)),
                     pl.BlockSpec((tm, LANES), lambda i: (i % tiles_per_seq, 0))]
        args += [cos, sin]
    else:
        out_shape += [jax.ShapeDtypeStruct((t, KV_W), F32), jax.ShapeDtypeStruct((t, KV_W), F32)]
        out_specs += [pl.BlockSpec((tm, KV_W), lambda i: (i, 0)),
                      pl.BlockSpec((tm, KV_W), lambda i: (i, 0))]
    return pl.pallas_call(
        functools.partial(_mixin_kernel, latent=latent),
        grid=(t // tm,),
        in_specs=in_specs, out_specs=out_specs, out_shape=out_shape,
        scratch_shapes=[pltpu.VMEM((D_MODEL, IN_W), BF16)],
        compiler_params=_cparams(1),
        name="mix_in_lat" if latent else "mix_in_ctx",
    )(*args)


def _attn_kernel(*refs, n_seq, latent):
    if latent:
        q_ref, kd_ref, vd_ref, ck_ref, cv_ref, o_ref = refs
    else:
        q_ref, kd_ref, vd_ref, o_ref = refs
    tq = min(Q_TILE, n_seq)
    low1 = lax.broadcasted_iota(jnp.int32, (1, LANES), 1) < HEAD_DIM
    low_q = lax.broadcasted_iota(jnp.int32, (tq, LANES), 1) < HEAD_DIM
    if latent:
        ck = ck_ref[0]
        cv = cv_ref[0]
        ckd, cvd = _attn_layouts(ck, pltpu.roll(ck, HEAD_DIM, 1), cv, pltpu.roll(cv, HEAD_DIM, 1))

    def body(i, carry):
        r0 = pl.multiple_of(i * tq, tq)
        for j in range(N_HEADS // 2):
            g = (2 * j) // (N_HEADS // N_KV_HEADS)
            qp = q_ref[pl.ds(r0, tq), j * LANES:(j + 1) * LANES]
            kg = kd_ref[:, g * LANES:(g + 1) * LANES]
            outs = []
            for half in range(2):
                sel = low1 if half == 0 else jnp.logical_not(low1)
                qm = jnp.where(sel, qp, jnp.zeros_like(qp))
                s1 = _nt_dot(qm, kg)
                mx = jnp.max(s1, axis=-1, keepdims=True)
                if latent:
                    s2 = _nt_dot(qm, ckd[g])
                    mx = jnp.maximum(mx, jnp.max(s2, axis=-1, keepdims=True))
                vv = vd_ref[:, (2 * g + half) * LANES:(2 * g + half + 1) * LANES]
                o = jnp.dot(jnp.exp2(s1 - mx).astype(BF16), vv, preferred_element_type=F32)
                if latent:
                    o = o + jnp.dot(jnp.exp2(s2 - mx).astype(BF16), cvd[2 * g + half],
                                    preferred_element_type=F32)
                d0 = HEAD_DIM * (1 - half)
                outs.append(o * (1.0 / o[:, d0:d0 + 1]))
            o_ref[pl.ds(r0, tq), j * LANES:(j + 1) * LANES] = jnp.where(low_q, outs[0], outs[1]).astype(BF16)
        return carry

    lax.fori_loop(0, n_seq // tq, body, 0)


def _attention(q, kd, vd, cache, *, n_seq, latent):
    t = q.shape[0]
    in_specs = [pl.BlockSpec((n_seq, ATTN_W), lambda b: (b, 0)),
                pl.BlockSpec((n_seq, 2 * LANES), lambda b: (b, 0)),
                pl.BlockSpec((n_seq, 4 * LANES), lambda b: (b, 0))]
    args = [q, kd, vd]
    if latent:
        ck, cv = cache
        in_specs += [pl.BlockSpec((1,) + ck.shape[1:], lambda b: (b, 0, 0)),
                     pl.BlockSpec((1,) + cv.shape[1:], lambda b: (b, 0, 0))]
        args += [ck, cv]
    return pl.pallas_call(
        functools.partial(_attn_kernel, n_seq=n_seq, latent=latent),
        grid=(t // n_seq,),
        in_specs=in_specs,
        out_specs=pl.BlockSpec((n_seq, ATTN_W), lambda b: (b, 0)),
        out_shape=jax.ShapeDtypeStruct((t, ATTN_W), BF16),
        compiler_params=_cparams(1),
        name="attn_lat" if latent else "attn_ctx",
    )(*args)


def _conv_kernel(u_ref, w_ref, cb_ref, lg_ref, lb_ref, o_ref, pad_ref, y_ref, *, n_seq):
    zeros = jnp.zeros((PAD_ROWS, D_CONV), F32)
    pad_ref[0:PAD_ROWS, :] = zeros
    pad_ref[PAD_ROWS + n_seq:2 * PAD_ROWS + n_seq, :] = zeros
    pad_ref[PAD_ROWS:PAD_ROWS + n_seq, :] = u_ref[...]
    base = PAD_ROWS - CONV_PAD
    rows = CONV_ROWS
    ext = rows + SUBLANES

    def conv_body(c, carry):
        r0 = pl.multiple_of(c * rows, rows)
        for j in range(D_CONV // LANES):
            ls = slice(j * LANES, (j + 1) * LANES)
            acc = jnp.zeros((rows, LANES), F32) + cb_ref[:, ls]
            for b in range(SUBLANES):
                z = None
                for a in range((CONV_WIDTH + base + SUBLANES - 1) // SUBLANES):
                    k = SUBLANES * a + b - base
                    if 0 <= k < CONV_WIDTH:
                        term = pad_ref[pl.ds(r0 + SUBLANES * a, ext), ls] * w_ref[k:k + 1, ls]
                        z = term if z is None else z + term
                if b:
                    z = pltpu.roll(z, ext - b, 0)
                acc = acc + z[0:rows, :]
            y_ref[pl.ds(r0, rows), ls] = acc
        return carry

    lax.fori_loop(0, n_seq // rows, conv_body, 0)

    def norm_body(c, carry):
        r0 = pl.multiple_of(c * LN_ROWS, LN_ROWS)
        acc = y_ref[pl.ds(r0, LN_ROWS), :]
        mu = jnp.mean(acc, axis=-1, keepdims=True)
        cen = acc - mu
        var = jnp.mean(cen * cen, axis=-1, keepdims=True)
        y = cen * lax.rsqrt(var + EPS) * lg_ref[...] + lb_ref[...]
        o_ref[pl.ds(r0, LN_ROWS), :] = (y * _sigmoid(y)).astype(BF16)
        return carry

    lax.fori_loop(0, n_seq // LN_ROWS, norm_body, 0)


def _conv(u, conv_w, conv_b, ln_g, ln_b, *, n_seq, name):
    t = u.shape[0]
    vec = pl.BlockSpec((1, D_CONV), lambda b: (0, 0))
    return pl.pallas_call(
        functools.partial(_conv_kernel, n_seq=n_seq),
        grid=(t // n_seq,),
        in_specs=[pl.BlockSpec((n_seq, D_CONV), lambda b: (b, 0)),
                  pl.BlockSpec((CONV_WIDTH, D_CONV), lambda b: (0, 0)), vec, vec, vec],
        out_specs=pl.BlockSpec((n_seq, D_CONV), lambda b: (b, 0)),
        out_shape=jax.ShapeDtypeStruct((t, D_CONV), BF16),
        scratch_shapes=[pltpu.VMEM((n_seq + 2 * PAD_ROWS, D_CONV), F32),
                        pltpu.VMEM((n_seq, D_CONV), F32)],
        compiler_params=_cparams(1),
        name=name,
    )(u, conv_w, conv_b, ln_g, ln_b)


def _mixout_kernel(attn_ref, conv_ref, x_ref, mod_ref, wo_ref, gpost_ref, gpre_ref, wr_ref,
                   x1_o, h2_o, aff_o, wo_bf, wr_hi, wr_lo, *, n_seq):
    tm = x_ref.shape[0]

    @pl.when(pl.program_id(0) == 0)
    def _():
        wo_bf[...] = wo_ref[...].astype(BF16)
        hi, lo = _split_bf16(wr_ref[...])
        wr_hi[...] = hi
        wr_lo[...] = lo

    m = mod_ref[0]
    ga1 = m[:, 2 * D_MODEL:3 * D_MODEL]
    sh2 = m[:, 3 * D_MODEL:4 * D_MODEL]
    sc2 = m[:, 4 * D_MODEL:5 * D_MODEL]
    sub = min(MIXOUT_SUB, n_seq)
    for s in range(tm // sub):
        rs = slice(s * sub, (s + 1) * sub)
        out = (jnp.dot(attn_ref[rs, :], wo_bf[0:ATTN_W, :], preferred_element_type=F32)
               + jnp.dot(conv_ref[rs, :], wo_bf[ATTN_W:, :], preferred_element_type=F32))
        ms = jnp.mean(out * out, axis=-1, keepdims=True)
        x1 = x_ref[rs, :] + ga1 * ((out * lax.rsqrt(ms + EPS)) * gpost_ref[...])
        x1_o[rs, :] = x1
        ms2 = jnp.mean(x1 * x1, axis=-1, keepdims=True)
        h2 = ((x1 * lax.rsqrt(ms2 + EPS)) * gpre_ref[...]) * (1.0 + sc2) + sh2
        h2_o[rs, :] = h2.astype(BF16)
        h_hi, h_lo = _split_bf16(h2)
        logits = (jnp.dot(h_hi, wr_hi[...], preferred_element_type=F32)
                  + jnp.dot(h_hi, wr_lo[...], preferred_element_type=F32)
                  + jnp.dot(h_lo, wr_hi[...], preferred_element_type=F32))
        lt = logits.T[0:N_EXPERTS, :]
        mx = jnp.max(lt, axis=0, keepdims=True)
        e = jnp.exp(lt - mx)
        aff = e / jnp.sum(e, axis=0, keepdims=True)
        t0 = s * sub
        aff_o[t0 // n_seq, :, t0 % n_seq:t0 % n_seq + sub] = aff


def _mix_out(attn, conv, x2d, mod3, w_out, g_post, g_pre_ffn, wr_pad, *, n_seq, mod_row0, latent):
    t = x2d.shape[0]
    tm = TOKEN_TILE
    n_b = t // n_seq
    tiles_per_seq = max(n_seq // tm, 1)
    seqs_per_tile = max(tm // n_seq, 1)
    if latent:
        mod_idx = lambda i: (mod_row0 + i // tiles_per_seq, 0, 0)
    else:
        mod_idx = lambda i: (mod_row0, 0, 0)
    aff_w = min(tm, n_seq)
    vec = pl.BlockSpec((1, D_MODEL), lambda i: (0, 0))
    return pl.pallas_call(
        functools.partial(_mixout_kernel, n_seq=n_seq),
        grid=(t // tm,),
        in_specs=[pl.BlockSpec((tm, ATTN_W), lambda i: (i, 0)),
                  pl.BlockSpec((tm, D_CONV), lambda i: (i, 0)),
                  pl.BlockSpec((tm, D_MODEL), lambda i: (i, 0)),
                  pl.BlockSpec((1, 1, mod3.shape[2]), mod_idx),
                  pl.BlockSpec((D_MODEL, D_MODEL), lambda i: (0, 0)),
                  vec, vec,
                  pl.BlockSpec((D_MODEL, LANES), lambda i: (0, 0))],
        out_specs=[pl.BlockSpec((tm, D_MODEL), lambda i: (i, 0)),
                   pl.BlockSpec((tm, D_MODEL), lambda i: (i, 0)),
                   pl.BlockSpec((seqs_per_tile, N_EXPERTS, aff_w),
                                lambda i: (i // tiles_per_seq, 0, i % tiles_per_seq))],
        out_shape=[jax.ShapeDtypeStruct((t, D_MODEL), F32),
                   jax.ShapeDtypeStruct((t, D_MODEL), BF16),
                   jax.ShapeDtypeStruct((n_b, N_EXPERTS, n_seq), F32)],
        scratch_shapes=[pltpu.VMEM((D_MODEL, D_MODEL), BF16),
                        pltpu.VMEM((D_MODEL, LANES), BF16),
                        pltpu.VMEM((D_MODEL, LANES), BF16)],
        compiler_params=_cparams(1),
        name="mix_out_lat" if latent else "mix_out_ctx",
    )(attn, conv, x2d, mod3, w_out, g_post, g_pre_ffn, wr_pad)


def _route_kernel(aff_ref, slot_o, slot_t_o, *, n_seq, cap):
    n_b = aff_ref.shape[0]
    rows = n_b * N_EXPERTS
    a = aff_ref[...].reshape(rows, n_seq)
    capf = float(cap)

    def count_ge(thr_bits):
        thr = lax.bitcast_convert_type(thr_bits, F32)
        return jnp.sum(jnp.where(a >= thr, 1.0, 0.0), axis=-1, keepdims=True)

    def bis(i, lo):
        cand = lo | jnp.left_shift(jnp.int32(1), 29 - i)
        return jnp.where(count_ge(cand) >= capf, cand, lo)

    lo = lax.fori_loop(0, 30, bis, jnp.zeros((rows, 1), jnp.int32))
    kth = lax.bitcast_convert_type(lo, F32)
    above = lax.bitcast_convert_type(lo + 1, F32)
    gt = a >= above
    eq = jnp.logical_and(a >= kth, jnp.logical_not(gt))
    need = capf - jnp.sum(jnp.where(gt, 1.0, 0.0), axis=-1, keepdims=True)
    ri = lax.broadcasted_iota(jnp.int32, (n_seq, n_seq), 0)
    ci = lax.broadcasted_iota(jnp.int32, (n_seq, n_seq), 1)
    upper = jnp.where(ri <= ci, 1.0, 0.0).astype(BF16)
    eq_f = jnp.where(eq, 1.0, 0.0)
    eq_rank = jnp.dot(eq_f.astype(BF16), upper, preferred_element_type=F32)
    sel_f = jnp.where(gt, 1.0, jnp.where(eq_rank <= need, eq_f, 0.0))
    pos = jnp.dot(sel_f.astype(BF16), upper, preferred_element_type=F32) - 1.0
    slot = jnp.where(sel_f > 0.5, pos, -1.0)
    slot_o[...] = slot.reshape(n_b, N_EXPERTS, n_seq)
    fill = jnp.full((LANES - N_EXPERTS, n_seq), -1.0, F32)
    for b in range(n_b):
        sb = jnp.concatenate([slot[b * N_EXPERTS:(b + 1) * N_EXPERTS, :], fill], axis=0)
        slot_t_o[b] = sb.T


def _route(aff, *, n_seq, cap, name):
    n_b = aff.shape[0]
    return pl.pallas_call(
        functools.partial(_route_kernel, n_seq=n_seq, cap=cap),
        out_shape=[jax.ShapeDtypeStruct((n_b, N_EXPERTS, n_seq), F32),
                   jax.ShapeDtypeStruct((n_b, n_seq, LANES), F32)],
        compiler_params=pltpu.CompilerParams(vmem_limit_bytes=VMEM_LIMIT),
        name=name,
    )(aff)


def _gather_kernel(slot_ref, aff_ref, h_ref, xs_o, gate_o, p_ref, *, n_seq, cap):
    slot = slot_ref[0]
    aff = aff_ref[0]
    srow = lax.broadcasted_iota(jnp.int32, (cap, n_seq), 0).astype(F32)
    for e in range(N_EXPERTS):
        hit = jnp.broadcast_to(slot[e:e + 1, :], (cap, n_seq)) == srow
        p_ref[e * cap:(e + 1) * cap, :] = jnp.where(hit, 1.0, 0.0).astype(BF16)
        ga = jnp.where(hit, jnp.broadcast_to(aff[e:e + 1, :], (cap, n_seq)), 0.0)
        gate_o[e] = jnp.sum(ga, axis=-1, keepdims=True)
    xs = jnp.dot(p_ref[...], h_ref[...], preferred_element_type=F32).astype(BF16)
    for e in range(N_EXPERTS):
        xs_o[e] = xs[e * cap:(e + 1) * cap, :]


def _gather(slot, aff, h2, *, n_seq, cap, name):
    n_b = slot.shape[0]
    return pl.pallas_call(
        functools.partial(_gather_kernel, n_seq=n_seq, cap=cap),
        grid=(n_b,),
        in_specs=[pl.BlockSpec((1, N_EXPERTS, n_seq), lambda b: (b, 0, 0)),
                  pl.BlockSpec((1, N_EXPERTS, n_seq), lambda b: (b, 0, 0)),
                  pl.BlockSpec((n_seq, D_MODEL), lambda b: (b, 0))],
        out_specs=[pl.BlockSpec((N_EXPERTS, cap, D_MODEL), lambda b: (0, b, 0)),
                   pl.BlockSpec((N_EXPERTS, cap, 1), lambda b: (0, b, 0))],
        out_shape=[jax.ShapeDtypeStruct((N_EXPERTS, n_b * cap, D_MODEL), BF16),
                   jax.ShapeDtypeStruct((N_EXPERTS, n_b * cap, 1), F32)],
        scratch_shapes=[pltpu.VMEM((N_EXPERTS * cap, n_seq), BF16)],
        compiler_params=_cparams(1),
        name=name,
    )(slot, aff, h2)


FF_CHUNK = 256


def _ffn_kernel(xa_ref, xb_ref, ga_ref, gb_ref, wg_ref, wu_ref, wd_ref, ya_o, yb_o):
    for x_ref, g_ref, y_o in ((xa_ref, ga_ref, ya_o), (xb_ref, gb_ref, yb_o)):
        x = x_ref[0]
        y = jnp.zeros((x.shape[0], D_MODEL), F32)
        for c in range(0, D_FF, FF_CHUNK):
            wg = wg_ref[0, 0, :, c:c + FF_CHUNK].astype(BF16)
            wu = wu_ref[0, 0, :, c:c + FF_CHUNK].astype(BF16)
            wd = wd_ref[0, 0, c:c + FF_CHUNK, :].astype(BF16)
            hg = jnp.dot(x, wg, preferred_element_type=F32)
            hu = jnp.dot(x, wu, preferred_element_type=F32)
            act = ((hg * _sigmoid(hg)) * hu).astype(BF16)
            y = y + jnp.dot(act, wd, preferred_element_type=F32)
        y_o[0] = (y * g_ref[0]).astype(BF16)


def _ffn(xs_a, xs_b, gate_a, gate_b, w_gate, w_up, w_down):
    ra, rb = xs_a.shape[1], xs_b.shape[1]
    wspec = lambda r, c: pl.BlockSpec((1, 1, r, c), lambda e: (0, e, 0, 0))
    return pl.pallas_call(
        _ffn_kernel,
        grid=(N_EXPERTS,),
        in_specs=[pl.BlockSpec((1, ra, D_MODEL), lambda e: (e, 0, 0)),
                  pl.BlockSpec((1, rb, D_MODEL), lambda e: (e, 0, 0)),
                  pl.BlockSpec((1, ra, 1), lambda e: (e, 0, 0)),
                  pl.BlockSpec((1, rb, 1), lambda e: (e, 0, 0)),
                  wspec(D_MODEL, D_FF), wspec(D_MODEL, D_FF), wspec(D_FF, D_MODEL)],
        out_specs=[pl.BlockSpec((1, ra, D_MODEL), lambda e: (e, 0, 0)),
                   pl.BlockSpec((1, rb, D_MODEL), lambda e: (e, 0, 0))],
        out_shape=[jax.ShapeDtypeStruct((N_EXPERTS, ra, D_MODEL), BF16),
                   jax.ShapeDtypeStruct((N_EXPERTS, rb, D_MODEL), BF16)],
        compiler_params=_cparams(1),
        name="expert_ffn",
    )(xs_a, xs_b, gate_a, gate_b, w_gate, w_up, w_down)


def _combine_kernel(slot_t_ref, ys_ref, x1_ref, mod_ref, g_ref, o_ref, *, n_seq, cap):
    n_slots = N_EXPERTS * cap
    st = slot_t_ref[0].astype(BF16)
    er = lax.broadcasted_iota(jnp.int32, (LANES, n_slots), 0)
    ec = lax.broadcasted_iota(jnp.int32, (LANES, n_slots), 1)
    expand = jnp.where(er == (ec >> (cap.bit_length() - 1)), 1.0, 0.0).astype(BF16)
    slot_x = jnp.dot(st, expand, preferred_element_type=F32)
    col_slot = (lax.broadcasted_iota(jnp.int32, (n_seq, n_slots), 1) & (cap - 1)).astype(F32)
    pt = jnp.where(slot_x == col_slot, 1.0, 0.0).astype(BF16)
    moe = jnp.dot(pt, ys_ref[...].reshape(n_slots, D_MODEL), preferred_element_type=F32)
    ga2 = mod_ref[0][:, 5 * D_MODEL:6 * D_MODEL]
    ms = jnp.mean(moe * moe, axis=-1, keepdims=True)
    o_ref[...] = x1_ref[...] + ga2 * ((moe * lax.rsqrt(ms + EPS)) * g_ref[...])


def _combine(slot_t, ys, x1, mod3, g_post_ffn, *, n_seq, cap, mod_row0, latent, name):
    n_b = slot_t.shape[0]
    if latent:
        mod_idx = lambda b: (mod_row0 + b, 0, 0)
    else:
        mod_idx = lambda b: (mod_row0, 0, 0)
    return pl.pallas_call(
        functools.partial(_combine_kernel, n_seq=n_seq, cap=cap),
        grid=(n_b,),
        in_specs=[pl.BlockSpec((1, n_seq, LANES), lambda b: (b, 0, 0)),
                  pl.BlockSpec((N_EXPERTS, cap, D_MODEL), lambda b: (0, b, 0)),
                  pl.BlockSpec((n_seq, D_MODEL), lambda b: (b, 0)),
                  pl.BlockSpec((1, 1, mod3.shape[2]), mod_idx),
                  pl.BlockSpec((1, D_MODEL), lambda b: (0, 0))],
        out_specs=pl.BlockSpec((n_seq, D_MODEL), lambda b: (b, 0)),
        out_shape=jax.ShapeDtypeStruct((n_b * n_seq, D_MODEL), F32),
        compiler_params=_cparams(1),
        name=name,
    )(slot_t, ys, x1, mod3, g_post_ffn)


def _rope_tables(n_seq):
    half = HEAD_DIM // 4
    freqs = ROPE_THETA ** (-np.arange(half, dtype=np.float64) / half)
    lane = np.arange(LANES)
    d = lane % HEAD_DIM
    j = d % half
    tok = np.arange(n_seq)
    pos_row = (tok // GRID_W).astype(np.float64)
    pos_col = (tok % GRID_W).astype(np.float64)
    pos = np.where((d < HEAD_DIM // 2)[None, :], pos_row[:, None], pos_col[:, None])
    ang = pos * freqs[j][None, :]
    sign = np.where((d % (2 * half)) < half, -1.0, 1.0)
    return (jnp.asarray(np.cos(ang), dtype=F32), jnp.asarray(np.sin(ang) * sign[None, :], dtype=F32))


def _path(x, mod3, mod_row0, latent, cache, wts):
    (g_pre_mix, g_post_mix, g_pre_ffn, g_post_ffn, w_in, qg, kg, bd, conv_w, conv_b, ln_g, ln_b,
     w_out, wr_pad) = wts
    n_b, n_seq, _ = x.shape
    x2d = x.reshape(n_b * n_seq, D_MODEL)
    rope_tabs = _rope_tables(n_seq) if latent else None
    outs = _mix_in(x2d, mod3, g_pre_mix, w_in, qg, kg, bd, rope_tabs,
                   n_seq=n_seq, mod_row0=mod_row0, latent=latent)
    q, kd, vd, u = outs[:4]
    tag = "lat" if latent else "ctx"
    attn = _attention(q, kd, vd, cache, n_seq=n_seq, latent=latent)
    conv = _conv(u, conv_w, conv_b, ln_g, ln_b, n_seq=n_seq, name="conv_" + tag)
    x1, h2, aff = _mix_out(attn, conv, x2d, mod3, w_out, g_post_mix, g_pre_ffn, wr_pad,
                           n_seq=n_seq, mod_row0=mod_row0, latent=latent)
    cap = CAPACITY_FACTOR * n_seq // N_EXPERTS
    slot, slot_t = _route(aff, n_seq=n_seq, cap=cap, name="route_" + tag)
    xs, gate = _gather(slot, aff, h2, n_seq=n_seq, cap=cap, name="gather_" + tag)
    return dict(x1=x1, slot_t=slot_t, xs=xs, gate=gate, cap=cap, n_seq=n_seq, n_b=n_b,
                kv=outs[4:], mod_row0=mod_row0, latent=latent, tag=tag)


def kernel(x_prompt, x_sample, c, c_ctx, cache_k, cache_v, w_mod, b_mod, g_pre_mix, g_post_mix,
           g_pre_ffn, g_post_ffn, w_in, q_norm, k_norm, conv_w, conv_b, conv_ln_g, conv_ln_b,
           w_out, w_router, w_gate, w_up, w_down):
    l = 0
    n_lat = x_sample.shape[0]
    cond8 = jnp.concatenate([c_ctx[None, :], c, jnp.zeros((8 - 1 - n_lat, D_MODEL), F32)], axis=0)
    mod = _modulation(cond8, w_mod[l], b_mod[l])
    mod3 = mod.reshape(8, 1, mod.shape[1])

    bd_r = jnp.arange(MXU_DIM) // HEAD_DIM
    bd = (bd_r[:, None] == bd_r[None, :]).astype(BF16)
    row = lambda v: v.reshape(1, -1)
    wr_pad = jnp.pad(w_router[l], ((0, 0), (0, LANES - N_EXPERTS)))
    wts = (row(g_pre_mix[l]), row(g_post_mix[l]), row(g_pre_ffn[l]), row(g_post_ffn[l]),
           w_in[l], row(jnp.tile(q_norm[l], N_HEADS)), row(jnp.tile(k_norm[l], N_KV_HEADS)),
           bd, conv_w[l], row(conv_b[l]), row(conv_ln_g[l]), row(conv_ln_b[l]),
           w_out[l], wr_pad)

    past = cache_k.shape[2]
    cache = (cache_k[:, l].reshape(n_lat, past, KV_W), cache_v[:, l].reshape(n_lat, past, KV_W))
    ctx = _path(x_prompt, mod3, 0, False, None, wts)
    lat = _path(x_sample, mod3, 1, True, cache, wts)

    ys_ctx, ys_lat = _ffn(ctx["xs"], lat["xs"], ctx["gate"], lat["gate"], w_gate, w_up, w_down)

    outs = []
    for p, ys, x in ((ctx, ys_ctx, x_prompt), (lat, ys_lat, x_sample)):
        y = _combine(p["slot_t"], ys, p["x1"], mod3, row(g_post_ffn[l]), n_seq=p["n_seq"], cap=p["cap"],
                     mod_row0=p["mod_row0"], latent=p["latent"], name="combine_" + p["tag"])
        outs.append(y.reshape(x.shape))
    k32, v32 = ctx["kv"]
    b, s = x_prompt.shape[0], x_prompt.shape[1]
    new_k = k32.reshape(b, 1, s, N_KV_HEADS, HEAD_DIM)
    new_v = v32.reshape(b, 1, s, N_KV_HEADS, HEAD_DIM)
    return (outs[0], outs[1], new_k, new_v)
```

```python
import functools
import math

import jax
import jax.numpy as jnp
import numpy as np
from jax import lax
from jax.experimental import pallas as pl
from jax.experimental.pallas import tpu as pltpu

F32 = jnp.float32
BF16 = jnp.bfloat16

D_MODEL = 1024
HEAD_DIM = 64
N_HEADS = 8
N_KV_HEADS = 2
ATTN_W = N_HEADS * HEAD_DIM
KV_W = N_KV_HEADS * HEAD_DIM
D_CONV = 512
IN_W = ATTN_W + 2 * KV_W + 2 * D_CONV
CONV_WIDTH = 31
CONV_PAD = CONV_WIDTH // 2
GRID_W = 64
ROPE_THETA = 10000.0
N_EXPERTS = 16
CAPACITY_FACTOR = 2
D_FF = 1024
EPS = 1e-6
LOG2_E = 1.4426950408889634

LANES = 128
SUBLANES = 8
MXU_DIM = 256
VMEM_LIMIT = 56 * 1024 * 1024

TOKEN_TILE = 512
Q_TILE = 256
CORE_BLOCK = 1024
MIXIN_SUB = 256
MIXOUT_SUB = 256
CONV_ROWS = 128
LN_ROWS = 64
PAD_ROWS = 16


def _cparams(n_axes, flags=None):
    return pltpu.CompilerParams(
        dimension_semantics=("arbitrary",) * n_axes, vmem_limit_bytes=VMEM_LIMIT, flags=flags)


def _sigmoid(x):
    return jax.nn.sigmoid(x)


def _nt_dot(a, b):
    return lax.dot_general(a, b, (((1,), (1,)), ((), ())), preferred_element_type=F32)


def _split_bf16(x):
    hi = x.astype(BF16)
    lo = (x - hi.astype(F32)).astype(BF16)
    return hi, lo


def _mod_kernel(cond_ref, w_ref, b_ref, o_ref):
    cnd = cond_ref[...]
    s = (cnd * _sigmoid(cnd)).astype(BF16)
    o_ref[...] = jnp.dot(s, w_ref[...].astype(BF16), preferred_element_type=F32) + b_ref[...]


def _modulation(cond8, w_mod, b_mod):
    n_out = w_mod.shape[1]
    blk = 1024
    return pl.pallas_call(
        _mod_kernel,
        grid=(n_out // blk,),
        in_specs=[pl.BlockSpec((8, D_MODEL), lambda j: (0, 0)),
                  pl.BlockSpec((D_MODEL, blk), lambda j: (0, j)),
                  pl.BlockSpec((1, blk), lambda j: (0, j))],
        out_specs=pl.BlockSpec((8, blk), lambda j: (0, j)),
        out_shape=jax.ShapeDtypeStruct((8, n_out), F32),
        compiler_params=_cparams(1),
        name="modulation",
    )(cond8, w_mod, b_mod.reshape(1, n_out))


def _head_norm(t, gain, bd_ref):
    w = t.shape[1]
    cw = min(w, MXU_DIM)
    t2 = t * t
    hi, lo = _split_bf16(t2)
    bd = bd_ref[:cw, :cw]
    parts = []
    for c in range(0, w, cw):
        parts.append(jnp.dot(hi[:, c:c + cw], bd, preferred_element_type=F32)
                     + jnp.dot(lo[:, c:c + cw], bd, preferred_element_type=F32))
    ss = parts[0] if len(parts) == 1 else jnp.concatenate(parts, axis=1)
    return t * lax.rsqrt(ss * (1.0 / HEAD_DIM) + EPS) * gain


def _rope(t, cos, sin_signed):
    rows = t.shape[0]
    lane = lax.broadcasted_iota(jnp.int32, (rows, LANES), 1)
    first_half = (lane & 31) < 16
    outs = []
    for c in range(0, t.shape[1], LANES):
        tc = t[:, c:c + LANES]
        fwd = pltpu.roll(tc, LANES - 16, 1)
        bwd = pltpu.roll(tc, 16, 1)
        partner = jnp.where(first_half, fwd, bwd)
        outs.append(tc * cos + partner * sin_signed)
    return outs[0] if len(outs) == 1 else jnp.concatenate(outs, axis=1)


def _attn_layouts(k, k_sw, v, v_sw):
    lane = lax.broadcasted_iota(jnp.int32, k.shape, 1)
    low = lane < HEAD_DIM
    one_hi = jnp.where(lane == HEAD_DIM, 1.0, 0.0)
    one_lo = jnp.where(lane == 0, 1.0, 0.0)
    kd = [jnp.where(low, k, k_sw).astype(BF16), jnp.where(low, k_sw, k).astype(BF16)]
    vd = [jnp.where(low, v, one_hi).astype(BF16), jnp.where(low, one_lo, v_sw).astype(BF16),
          jnp.where(low, v_sw, one_hi).astype(BF16), jnp.where(low, one_lo, v).astype(BF16)]
    return kd, vd


def _mixin_kernel(*refs, latent):
    if latent:
        (x_ref, mod_ref, g_ref, w_ref, qg_ref, kg_ref, bd_ref, cos_ref, sin_ref,
         q_o, kd_o, vd_o, u_o, w_bf) = refs
    else:
        (x_ref, mod_ref, g_ref, w_ref, qg_ref, kg_ref, bd_ref,
         q_o, kd_o, vd_o, u_o, k32_o, v32_o, w_bf) = refs

    @pl.when(pl.program_id(0) == 0)
    def _():
        w_bf[...] = w_ref[...].astype(BF16)

    m = mod_ref[0]
    sh1 = m[:, 0:D_MODEL]
    sc1 = m[:, D_MODEL:2 * D_MODEL]
    tm = x_ref.shape[0]
    for s in range(tm // MIXIN_SUB):
        rs = slice(s * MIXIN_SUB, (s + 1) * MIXIN_SUB)
        x = x_ref[rs, :]
        ms = jnp.mean(x * x, axis=-1, keepdims=True)
        h = (x * lax.rsqrt(ms + EPS)) * g_ref[...]
        h = h * (1.0 + sc1) + sh1
        p = jnp.dot(h.astype(BF16), w_bf[...], preferred_element_type=F32)
        q = _head_norm(p[:, :ATTN_W], qg_ref[...], bd_ref)
        k = _head_norm(p[:, ATTN_W:ATTN_W + KV_W], kg_ref[...], bd_ref)
        v = p[:, ATTN_W + KV_W:ATTN_W + 2 * KV_W]
        a = p[:, ATTN_W + 2 * KV_W:ATTN_W + 2 * KV_W + D_CONV]
        gt = p[:, ATTN_W + 2 * KV_W + D_CONV:]
        if latent:
            cos = cos_ref[rs, :]
            sin = sin_ref[rs, :]
            q = _rope(q, cos, sin)
            k_att = _rope(k, cos, sin)
        else:
            k32_o[rs, :] = k
            v32_o[rs, :] = v
            k_att = k
        q_o[rs, :] = (q * (LOG2_E / math.sqrt(HEAD_DIM))).astype(BF16)
        k_sw = pltpu.roll(k_att, HEAD_DIM, 1)
        v_sw = pltpu.roll(v, HEAD_DIM, 1)
        kd, vd = _attn_layouts(k_att, k_sw, v, v_sw)
        for i, t in enumerate(kd):
            kd_o[rs, i * LANES:(i + 1) * LANES] = t
        for i, t in enumerate(vd):
            vd_o[rs, i * LANES:(i + 1) * LANES] = t
        u_o[rs, :] = a * _sigmoid(gt)


def _mix_in(x2d, mod3, g_pre, w_in_bf, qg, kg, bd, rope_tabs, *, n_seq, mod_row0, latent):
    t = x2d.shape[0]
    tm = TOKEN_TILE
    tiles_per_seq = max(n_seq // tm, 1)
    if latent:
        mod_idx = lambda i: (mod_row0 + i // tiles_per_seq, 0, 0)
    else:
        mod_idx = lambda i: (mod_row0, 0, 0)
    in_specs = [pl.BlockSpec((tm, D_MODEL), lambda i: (i, 0)),
                pl.BlockSpec((1, 1, mod3.shape[2]), mod_idx),
                pl.BlockSpec((1, D_MODEL), lambda i: (0, 0)),
                pl.BlockSpec((D_MODEL, IN_W), lambda i: (0, 0)),
                pl.BlockSpec((1, ATTN_W), lambda i: (0, 0)),
                pl.BlockSpec((1, KV_W), lambda i: (0, 0)),
                pl.BlockSpec((MXU_DIM, MXU_DIM), lambda i: (0, 0))]
    args = [x2d, mod3, g_pre, w_in_bf, qg, kg, bd]
    out_shape = [jax.ShapeDtypeStruct((t, ATTN_W), BF16),
                 jax.ShapeDtypeStruct((t, 2 * LANES), BF16),
                 jax.ShapeDtypeStruct((t, 4 * LANES), BF16),
                 jax.ShapeDtypeStruct((t, D_CONV), F32)]
    out_specs = [pl.BlockSpec((tm, ATTN_W), lambda i: (i, 0)),
                 pl.BlockSpec((tm, 2 * LANES), lambda i: (i, 0)),
                 pl.BlockSpec((tm, 4 * LANES), lambda i: (i, 0)),
                 pl.BlockSpec((tm, D_CONV), lambda i: (i, 0))]
    if latent:
        cos, sin = rope_tabs
        in_specs += [pl.BlockSpec((tm, LANES), lambda i: (i % tiles_per_seq, 0)),
                     pl.BlockSpec((tm, LANES), lambda i: (i % tiles_per_seq, 0))]
        args += [cos, sin]
    else:
        out_shape += [jax.ShapeDtypeStruct((t, KV_W), F32), jax.ShapeDtypeStruct((t, KV_W), F32)]
        out_specs += [pl.BlockSpec((tm, KV_W), lambda i: (i, 0)),
                      pl.BlockSpec((tm, KV_W), lambda i: (i, 0))]
    return pl.pallas_call(
        functools.partial(_mixin_kernel, latent=latent),
        grid=(t // tm,),
        in_specs=in_specs, out_specs=out_specs, out_shape=out_shape,
        scratch_shapes=[pltpu.VMEM((D_MODEL, IN_W), BF16)],
        compiler_params=_cparams(1),
        name="mix_in_lat" if latent else "mix_in_ctx",
    )(*args)


def _core_kernel(*refs, n_seq, latent):
    if latent:
        (q_ref, kd_ref, vd_ref, u_ref, ck_ref, cv_ref, w_ref, cb_ref, lg_ref, lb_ref,
         attn_o, conv_o, pad_ref, y_ref) = refs
    else:
        (q_ref, kd_ref, vd_ref, u_ref, w_ref, cb_ref, lg_ref, lb_ref,
         attn_o, conv_o, pad_ref, y_ref) = refs
    blk = q_ref.shape[0]
    tq = Q_TILE
    seqs = blk // n_seq
    seg = n_seq + 2 * PAD_ROWS
    zeros = jnp.zeros((PAD_ROWS, D_CONV), F32)
    for s in range(seqs):
        pad_ref[s * seg:s * seg + PAD_ROWS, :] = zeros
        pad_ref[s * seg + PAD_ROWS + n_seq:(s + 1) * seg, :] = zeros
        pad_ref[s * seg + PAD_ROWS:s * seg + PAD_ROWS + n_seq, :] = u_ref[s * n_seq:(s + 1) * n_seq, :]

    low1 = lax.broadcasted_iota(jnp.int32, (1, LANES), 1) < HEAD_DIM
    low_q = lax.broadcasted_iota(jnp.int32, (tq, LANES), 1) < HEAD_DIM
    if latent:
        ck = ck_ref[0]
        cv = cv_ref[0]
        ckd, cvd = _attn_layouts(ck, pltpu.roll(ck, HEAD_DIM, 1), cv, pltpu.roll(cv, HEAD_DIM, 1))
    base = PAD_ROWS - CONV_PAD
    ext = CONV_ROWS + SUBLANES

    def attn_pair(r0, keys, j):
        g = (2 * j) // (N_HEADS // N_KV_HEADS)
        qp = q_ref[pl.ds(r0, tq), j * LANES:(j + 1) * LANES]
        kg = kd_ref[keys, g * LANES:(g + 1) * LANES]
        outs = []
        for half in range(2):
            sel = low1 if half == 0 else jnp.logical_not(low1)
            qm = jnp.where(sel, qp, jnp.zeros_like(qp))
            s1 = _nt_dot(qm, kg)
            mx = jnp.max(s1, axis=-1, keepdims=True)
            if latent:
                s2 = _nt_dot(qm, ckd[g])
                mx = jnp.maximum(mx, jnp.max(s2, axis=-1, keepdims=True))
            vv = vd_ref[keys, (2 * g + half) * LANES:(2 * g + half + 1) * LANES]
            o = jnp.dot(jnp.exp2(s1 - mx).astype(BF16), vv, preferred_element_type=F32)
            if latent:
                o = o + jnp.dot(jnp.exp2(s2 - mx).astype(BF16), cvd[2 * g + half],
                                preferred_element_type=F32)
            d0 = HEAD_DIM * (1 - half)
            outs.append(o * (1.0 / o[:, d0:d0 + 1]))
        attn_o[pl.ds(r0, tq), j * LANES:(j + 1) * LANES] = jnp.where(low_q, outs[0], outs[1]).astype(BF16)

    def conv_tile(p0, c, j):
        ls = slice(j * LANES, (j + 1) * LANES)
        acc = jnp.zeros((CONV_ROWS, LANES), F32) + cb_ref[:, ls]
        for b in range(SUBLANES):
            z = None
            for a in range((CONV_WIDTH + base + SUBLANES - 1) // SUBLANES):
                k = SUBLANES * a + b - base
                if 0 <= k < CONV_WIDTH:
                    rows = pl.ds(p0 + c * CONV_ROWS + SUBLANES * a, ext)
                    term = pad_ref[rows, ls] * w_ref[k:k + 1, ls]
                    z = term if z is None else z + term
            if b:
                z = pltpu.roll(z, ext - b, 0)
            acc = acc + z[0:CONV_ROWS, :]
        y_ref[c * CONV_ROWS:(c + 1) * CONV_ROWS, ls] = acc

    def norm_rows(r0, c):
        acc = y_ref[c * LN_ROWS:(c + 1) * LN_ROWS, :]
        mu = jnp.mean(acc, axis=-1, keepdims=True)
        cen = acc - mu
        var = jnp.mean(cen * cen, axis=-1, keepdims=True)
        y = cen * lax.rsqrt(var + EPS) * lg_ref[...] + lb_ref[...]
        conv_o[pl.ds(r0 + c * LN_ROWS, LN_ROWS), :] = (y * _sigmoid(y)).astype(BF16)

    def body(i, carry):
        r0 = pl.multiple_of(i * tq, tq)
        if latent:
            keys = slice(None)
            p0 = r0
        else:
            keys = pl.ds(r0, n_seq)
            p0 = pl.multiple_of(i * seg, SUBLANES)
        for j in range(N_HEADS // 2):
            attn_pair(r0, keys, j)
            for c in range(tq // CONV_ROWS):
                conv_tile(p0, c, j)
        for c in range(tq // LN_ROWS):
            norm_rows(r0, c)
        return carry

    lax.fori_loop(0, blk // tq, body, 0)


def _core(q, kd, vd, u, cache, conv_w, conv_b, ln_g, ln_b, *, n_seq, latent):
    t = q.shape[0]
    blk = max(n_seq, CORE_BLOCK)
    assert n_seq in (Q_TILE, blk), "a block is one sequence, or whole sequences of one query tile each"
    row = lambda w: pl.BlockSpec((blk, w), lambda b: (b, 0))
    vec = pl.BlockSpec((1, D_CONV), lambda b: (0, 0))
    in_specs = [row(ATTN_W), row(2 * LANES), row(4 * LANES), row(D_CONV)]
    args = [q, kd, vd, u]
    if latent:
        ck, cv = cache
        in_specs += [pl.BlockSpec((1,) + ck.shape[1:], lambda b: (b, 0, 0)),
                     pl.BlockSpec((1,) + cv.shape[1:], lambda b: (b, 0, 0))]
        args += [ck, cv]
    in_specs += [pl.BlockSpec((CONV_WIDTH, D_CONV), lambda b: (0, 0)), vec, vec, vec]
    args += [conv_w, conv_b, ln_g, ln_b]
    pad_rows = (blk // n_seq) * (n_seq + 2 * PAD_ROWS)
    return pl.pallas_call(
        functools.partial(_core_kernel, n_seq=n_seq, latent=latent),
        grid=(t // blk,),
        in_specs=in_specs,
        out_specs=[row(ATTN_W), row(D_CONV)],
        out_shape=[jax.ShapeDtypeStruct((t, ATTN_W), BF16), jax.ShapeDtypeStruct((t, D_CONV), BF16)],
        scratch_shapes=[pltpu.VMEM((pad_rows, D_CONV), F32),
                        pltpu.VMEM((Q_TILE, D_CONV), F32)],
        compiler_params=_cparams(1),
        name="core_lat" if latent else "core_ctx",
    )(*args)


def _mixout_kernel(attn_ref, conv_ref, x_ref, mod_ref, wo_ref, gpost_ref, gpre_ref, wr_ref,
                   x1_o, h2_o, aff_o, wo_bf, wr_hi, wr_lo, *, n_seq):
    tm = x_ref.shape[0]

    @pl.when(pl.program_id(0) == 0)
    def _():
        wo_bf[...] = wo_ref[...].astype(BF16)
        hi, lo = _split_bf16(wr_ref[...])
        wr_hi[...] = hi
        wr_lo[...] = lo

    m = mod_ref[0]
    ga1 = m[:, 2 * D_MODEL:3 * D_MODEL]
    sh2 = m[:, 3 * D_MODEL:4 * D_MODEL]
    sc2 = m[:, 4 * D_MODEL:5 * D_MODEL]
    sub = min(MIXOUT_SUB, n_seq)
    for s in range(tm // sub):
        rs = slice(s * sub, (s + 1) * sub)
        out = (jnp.dot(attn_ref[rs, :], wo_bf[0:ATTN_W, :], preferred_element_type=F32)
               + jnp.dot(conv_ref[rs, :], wo_bf[ATTN_W:, :], preferred_element_type=F32))
        ms = jnp.mean(out * out, axis=-1, keepdims=True)
        x1 = x_ref[rs, :] + ga1 * ((out * lax.rsqrt(ms + EPS)) * gpost_ref[...])
        x1_o[rs, :] = x1
        ms2 = jnp.mean(x1 * x1, axis=-1, keepdims=True)
        h2 = ((x1 * lax.rsqrt(ms2 + EPS)) * gpre_ref[...]) * (1.0 + sc2) + sh2
        h2_o[rs, :] = h2.astype(BF16)
        h_hi, h_lo = _split_bf16(h2)
        logits = (jnp.dot(h_hi, wr_hi[...], preferred_element_type=F32)
                  + jnp.dot(h_hi, wr_lo[...], preferred_element_type=F32)
                  + jnp.dot(h_lo, wr_hi[...], preferred_element_type=F32))
        lt = logits.T[0:N_EXPERTS, :]
        mx = jnp.max(lt, axis=0, keepdims=True)
        e = jnp.exp(lt - mx)
        aff = e / jnp.sum(e, axis=0, keepdims=True)
        t0 = s * sub
        aff_o[t0 // n_seq, :, t0 % n_seq:t0 % n_seq + sub] = aff


def _mix_out(attn, conv, x2d, mod3, w_out, g_post, g_pre_ffn, wr_pad, *, n_seq, mod_row0, latent):
    t = x2d.shape[0]
    tm = TOKEN_TILE
    n_b = t // n_seq
    tiles_per_seq = max(n_seq // tm, 1)
    seqs_per_tile = max(tm // n_seq, 1)
    if latent:
        mod_idx = lambda i: (mod_row0 + i // tiles_per_seq, 0, 0)
    else:
        mod_idx = lambda i: (mod_row0, 0, 0)
    aff_w = min(tm, n_seq)
    vec = pl.BlockSpec((1, D_MODEL), lambda i: (0, 0))
    return pl.pallas_call(
        functools.partial(_mixout_kernel, n_seq=n_seq),
        grid=(t // tm,),
        in_specs=[pl.BlockSpec((tm, ATTN_W), lambda i: (i, 0)),
                  pl.BlockSpec((tm, D_CONV), lambda i: (i, 0)),
                  pl.BlockSpec((tm, D_MODEL), lambda i: (i, 0)),
                  pl.BlockSpec((1, 1, mod3.shape[2]), mod_idx),
                  pl.BlockSpec((D_MODEL, D_MODEL), lambda i: (0, 0)),
                  vec, vec,
                  pl.BlockSpec((D_MODEL, LANES), lambda i: (0, 0))],
        out_specs=[pl.BlockSpec((tm, D_MODEL), lambda i: (i, 0)),
                   pl.BlockSpec((tm, D_MODEL), lambda i: (i, 0)),
                   pl.BlockSpec((seqs_per_tile, N_EXPERTS, aff_w),
                                lambda i: (i // tiles_per_seq, 0, i % tiles_per_seq))],
        out_shape=[jax.ShapeDtypeStruct((t, D_MODEL), F32),
                   jax.ShapeDtypeStruct((t, D_MODEL), BF16),
                   jax.ShapeDtypeStruct((n_b, N_EXPERTS, n_seq), F32)],
        scratch_shapes=[pltpu.VMEM((D_MODEL, D_MODEL), BF16),
                        pltpu.VMEM((D_MODEL, LANES), BF16),
                        pltpu.VMEM((D_MODEL, LANES), BF16)],
        compiler_params=_cparams(1),
        name="mix_out_lat" if latent else "mix_out_ctx",
    )(attn, conv, x2d, mod3, w_out, g_post, g_pre_ffn, wr_pad)


def _route_kernel(aff_ref, slot_o, slot_t_o, *, n_seq, cap):
    n_b = aff_ref.shape[0]
    rows = n_b * N_EXPERTS
    a = aff_ref[...].reshape(rows, n_seq)
    capf = float(cap)

    def count_ge(thr_bits):
        thr = lax.bitcast_convert_type(thr_bits, F32)
        return jnp.sum(jnp.where(a >= thr, 1.0, 0.0), axis=-1, keepdims=True)

    def bis(i, lo):
        cand = lo | jnp.left_shift(jnp.int32(1), 29 - i)
        return jnp.where(count_ge(cand) >= capf, cand, lo)

    lo = lax.fori_loop(0, 30, bis, jnp.zeros((rows, 1), jnp.int32))
    kth = lax.bitcast_convert_type(lo, F32)
    above = lax.bitcast_convert_type(lo + 1, F32)
    gt = a >= above
    eq = jnp.logical_and(a >= kth, jnp.logical_not(gt))
    need = capf - jnp.sum(jnp.where(gt, 1.0, 0.0), axis=-1, keepdims=True)
    ri = lax.broadcasted_iota(jnp.int32, (n_seq, n_seq), 0)
    ci = lax.broadcasted_iota(jnp.int32, (n_seq, n_seq), 1)
    upper = jnp.where(ri <= ci, 1.0, 0.0).astype(BF16)
    eq_f = jnp.where(eq, 1.0, 0.0)
    eq_rank = jnp.dot(eq_f.astype(BF16), upper, preferred_element_type=F32)
    sel_f = jnp.where(gt, 1.0, jnp.where(eq_rank <= need, eq_f, 0.0))
    pos = jnp.dot(sel_f.astype(BF16), upper, preferred_element_type=F32) - 1.0
    slot = jnp.where(sel_f > 0.5, pos, -1.0)
    slot_o[...] = slot.reshape(n_b, N_EXPERTS, n_seq)
    fill = jnp.full((LANES - N_EXPERTS, n_seq), -1.0, F32)
    for b in range(n_b):
        sb = jnp.concatenate([slot[b * N_EXPERTS:(b + 1) * N_EXPERTS, :], fill], axis=0)
        slot_t_o[b] = sb.T


def _route(aff, *, n_seq, cap, name):
    n_b = aff.shape[0]
    return pl.pallas_call(
        functools.partial(_route_kernel, n_seq=n_seq, cap=cap),
        out_shape=[jax.ShapeDtypeStruct((n_b, N_EXPERTS, n_seq), F32),
                   jax.ShapeDtypeStruct((n_b, n_seq, LANES), F32)],
        compiler_params=pltpu.CompilerParams(vmem_limit_bytes=VMEM_LIMIT),
        name=name,
    )(aff)


def _gather_kernel(slot_ref, aff_ref, h_ref, xs_o, gate_o, p_ref, *, n_seq, cap):
    slot = slot_ref[0]
    aff = aff_ref[0]
    srow = lax.broadcasted_iota(jnp.int32, (cap, n_seq), 0).astype(F32)
    for e in range(N_EXPERTS):
        hit = jnp.broadcast_to(slot[e:e + 1, :], (cap, n_seq)) == srow
        p_ref[e * cap:(e + 1) * cap, :] = jnp.where(hit, 1.0, 0.0).astype(BF16)
        ga = jnp.where(hit, jnp.broadcast_to(aff[e:e + 1, :], (cap, n_seq)), 0.0)
        gate_o[e] = jnp.sum(ga, axis=-1, keepdims=True)
    xs = jnp.dot(p_ref[...], h_ref[...], preferred_element_type=F32).astype(BF16)
    for e in range(N_EXPERTS):
        xs_o[e] = xs[e * cap:(e + 1) * cap, :]


def _gather(slot, aff, h2, *, n_seq, cap, name):
    n_b = slot.shape[0]
    return pl.pallas_call(
        functools.partial(_gather_kernel, n_seq=n_seq, cap=cap),
        grid=(n_b,),
        in_specs=[pl.BlockSpec((1, N_EXPERTS, n_seq), lambda b: (b, 0, 0)),
                  pl.BlockSpec((1, N_EXPERTS, n_seq), lambda b: (b, 0, 0)),
                  pl.BlockSpec((n_seq, D_MODEL), lambda b: (b, 0))],
        out_specs=[pl.BlockSpec((N_EXPERTS, cap, D_MODEL), lambda b: (0, b, 0)),
                   pl.BlockSpec((N_EXPERTS, cap, 1), lambda b: (0, b, 0))],
        out_shape=[jax.ShapeDtypeStruct((N_EXPERTS, n_b * cap, D_MODEL), BF16),
                   jax.ShapeDtypeStruct((N_EXPERTS, n_b * cap, 1), F32)],
        scratch_shapes=[pltpu.VMEM((N_EXPERTS * cap, n_seq), BF16)],
        compiler_params=_cparams(1),
        name=name,
    )(slot, aff, h2)


FF_CHUNK = 256


def _ffn_kernel(xa_ref, xb_ref, ga_ref, gb_ref, wg_ref, wu_ref, wd_ref, ya_o, yb_o):
    for x_ref, g_ref, y_o in ((xa_ref, ga_ref, ya_o), (xb_ref, gb_ref, yb_o)):
        x = x_ref[0]
        y = jnp.zeros((x.shape[0], D_MODEL), F32)
        for c in range(0, D_FF, FF_CHUNK):
            wg = wg_ref[0, 0, :, c:c + FF_CHUNK].astype(BF16)
            wu = wu_ref[0, 0, :, c:c + FF_CHUNK].astype(BF16)
            wd = wd_ref[0, 0, c:c + FF_CHUNK, :].astype(BF16)
            hg = jnp.dot(x, wg, preferred_element_type=F32)
            hu = jnp.dot(x, wu, preferred_element_type=F32)
            act = ((hg * _sigmoid(hg)) * hu).astype(BF16)
            y = y + jnp.dot(act, wd, preferred_element_type=F32)
        y_o[0] = (y * g_ref[0]).astype(BF16)


def _ffn(xs_a, xs_b, gate_a, gate_b, w_gate, w_up, w_down):
    ra, rb = xs_a.shape[1], xs_b.shape[1]
    wspec = lambda r, c: pl.BlockSpec((1, 1, r, c), lambda e: (0, e, 0, 0))
    return pl.pallas_call(
        _ffn_kernel,
        grid=(N_EXPERTS,),
        in_specs=[pl.BlockSpec((1, ra, D_MODEL), lambda e: (e, 0, 0)),
                  pl.BlockSpec((1, rb, D_MODEL), lambda e: (e, 0, 0)),
                  pl.BlockSpec((1, ra, 1), lambda e: (e, 0, 0)),
                  pl.BlockSpec((1, rb, 1), lambda e: (e, 0, 0)),
                  wspec(D_MODEL, D_FF), wspec(D_MODEL, D_FF), wspec(D_FF, D_MODEL)],
        out_specs=[pl.BlockSpec((1, ra, D_MODEL), lambda e: (e, 0, 0)),
                   pl.BlockSpec((1, rb, D_MODEL), lambda e: (e, 0, 0))],
        out_shape=[jax.ShapeDtypeStruct((N_EXPERTS, ra, D_MODEL), BF16),
                   jax.ShapeDtypeStruct((N_EXPERTS, rb, D_MODEL), BF16)],
        compiler_params=_cparams(1),
        name="expert_ffn",
    )(xs_a, xs_b, gate_a, gate_b, w_gate, w_up, w_down)


def _combine_kernel(slot_t_ref, ys_ref, x1_ref, mod_ref, g_ref, o_ref, *, n_seq, cap):
    n_slots = N_EXPERTS * cap
    st = slot_t_ref[0].astype(BF16)
    er = lax.broadcasted_iota(jnp.int32, (LANES, n_slots), 0)
    ec = lax.broadcasted_iota(jnp.int32, (LANES, n_slots), 1)
    expand = jnp.where(er == (ec >> (cap.bit_length() - 1)), 1.0, 0.0).astype(BF16)
    slot_x = jnp.dot(st, expand, preferred_element_type=F32)
    col_slot = (lax.broadcasted_iota(jnp.int32, (n_seq, n_slots), 1) & (cap - 1)).astype(F32)
    pt = jnp.where(slot_x == col_slot, 1.0, 0.0).astype(BF16)
    moe = jnp.dot(pt, ys_ref[...].reshape(n_slots, D_MODEL), preferred_element_type=F32)
    ga2 = mod_ref[0][:, 5 * D_MODEL:6 * D_MODEL]
    ms = jnp.mean(moe * moe, axis=-1, keepdims=True)
    o_ref[...] = x1_ref[...] + ga2 * ((moe * lax.rsqrt(ms + EPS)) * g_ref[...])


def _combine(slot_t, ys, x1, mod3, g_post_ffn, *, n_seq, cap, mod_row0, latent, name):
    n_b = slot_t.shape[0]
    if latent:
        mod_idx = lambda b: (mod_row0 + b, 0, 0)
    else:
        mod_idx = lambda b: (mod_row0, 0, 0)
    return pl.pallas_call(
        functools.partial(_combine_kernel, n_seq=n_seq, cap=cap),
        grid=(n_b,),
        in_specs=[pl.BlockSpec((1, n_seq, LANES), lambda b: (b, 0, 0)),
                  pl.BlockSpec((N_EXPERTS, cap, D_MODEL), lambda b: (0, b, 0)),
                  pl.BlockSpec((n_seq, D_MODEL), lambda b: (b, 0)),
                  pl.BlockSpec((1, 1, mod3.shape[2]), mod_idx),
                  pl.BlockSpec((1, D_MODEL), lambda b: (0, 0))],
        out_specs=pl.BlockSpec((n_seq, D_MODEL), lambda b: (b, 0)),
        out_shape=jax.ShapeDtypeStruct((n_b * n_seq, D_MODEL), F32),
        compiler_params=_cparams(1),
        name=name,
    )(slot_t, ys, x1, mod3, g_post_ffn)


def _rope_tables(n_seq):
    half = HEAD_DIM // 4
    freqs = ROPE_THETA ** (-np.arange(half, dtype=np.float64) / half)
    lane = np.arange(LANES)
    d = lane % HEAD_DIM
    j = d % half
    tok = np.arange(n_seq)
    pos_row = (tok // GRID_W).astype(np.float64)
    pos_col = (tok % GRID_W).astype(np.float64)
    pos = np.where((d < HEAD_DIM // 2)[None, :], pos_row[:, None], pos_col[:, None])
    ang = pos * freqs[j][None, :]
    sign = np.where((d % (2 * half)) < half, -1.0, 1.0)
    return (jnp.asarray(np.cos(ang), dtype=F32), jnp.asarray(np.sin(ang) * sign[None, :], dtype=F32))


def _path(x, mod3, mod_row0, latent, cache, wts):
    (g_pre_mix, g_post_mix, g_pre_ffn, g_post_ffn, w_in, qg, kg, bd, conv_w, conv_b, ln_g, ln_b,
     w_out, wr_pad) = wts
    n_b, n_seq, _ = x.shape
    x2d = x.reshape(n_b * n_seq, D_MODEL)
    rope_tabs = _rope_tables(n_seq) if latent else None
    outs = _mix_in(x2d, mod3, g_pre_mix, w_in, qg, kg, bd, rope_tabs,
                   n_seq=n_seq, mod_row0=mod_row0, latent=latent)
    q, kd, vd, u = outs[:4]
    tag = "lat" if latent else "ctx"
    attn, conv = _core(q, kd, vd, u, cache, conv_w, conv_b, ln_g, ln_b, n_seq=n_seq, latent=latent)
    x1, h2, aff = _mix_out(attn, conv, x2d, mod3, w_out, g_post_mix, g_pre_ffn, wr_pad,
                           n_seq=n_seq, mod_row0=mod_row0, latent=latent)
    cap = CAPACITY_FACTOR * n_seq // N_EXPERTS
    slot, slot_t = _route(aff, n_seq=n_seq, cap=cap, name="route_" + tag)
    xs, gate = _gather(slot, aff, h2, n_seq=n_seq, cap=cap, name="gather_" + tag)
    return dict(x1=x1, slot_t=slot_t, xs=xs, gate=gate, cap=cap, n_seq=n_seq, n_b=n_b,
                kv=outs[4:], mod_row0=mod_row0, latent=latent, tag=tag)


def kernel(x_prompt, x_sample, c, c_ctx, cache_k, cache_v, w_mod, b_mod, g_pre_mix, g_post_mix,
           g_pre_ffn, g_post_ffn, w_in, q_norm, k_norm, conv_w, conv_b, conv_ln_g, conv_ln_b,
           w_out, w_router, w_gate, w_up, w_down):
    l = 0
    n_lat = x_sample.shape[0]
    cond8 = jnp.concatenate([c_ctx[None, :], c, jnp.zeros((8 - 1 - n_lat, D_MODEL), F32)], axis=0)
    mod = _modulation(cond8, w_mod[l], b_mod[l])
    mod3 = mod.reshape(8, 1, mod.shape[1])

    bd_r = jnp.arange(MXU_DIM) // HEAD_DIM
    bd = (bd_r[:, None] == bd_r[None, :]).astype(BF16)
    row = lambda v: v.reshape(1, -1)
    wr_pad = jnp.pad(w_router[l], ((0, 0), (0, LANES - N_EXPERTS)))
    wts = (row(g_pre_mix[l]), row(g_post_mix[l]), row(g_pre_ffn[l]), row(g_post_ffn[l]),
           w_in[l], row(jnp.tile(q_norm[l], N_HEADS)), row(jnp.tile(k_norm[l], N_KV_HEADS)),
           bd, conv_w[l], row(conv_b[l]), row(conv_ln_g[l]), row(conv_ln_b[l]),
           w_out[l], wr_pad)

    past = cache_k.shape[2]
    cache = (cache_k[:, l].reshape(n_lat, past, KV_W), cache_v[:, l].reshape(n_lat, past, KV_W))
    ctx = _path(x_prompt, mod3, 0, False, None, wts)
    lat = _path(x_sample, mod3, 1, True, cache, wts)

    ys_ctx, ys_lat = _ffn(ctx["xs"], lat["xs"], ctx["gate"], lat["gate"], w_gate, w_up, w_down)

    outs = []
    for p, ys, x in ((ctx, ys_ctx, x_prompt), (lat, ys_lat, x_sample)):
        y = _combine(p["slot_t"], ys, p["x1"], mod3, row(g_post_ffn[l]), n_seq=p["n_seq"], cap=p["cap"],
                     mod_row0=p["mod_row0"], latent=p["latent"], name="combine_" + p["tag"])
        outs.append(y.reshape(x.shape))
    k32, v32 = ctx["kv"]
    b, s = x_prompt.shape[0], x_prompt.shape[1]
    new_k = k32.reshape(b, 1, s, N_KV_HEADS, HEAD_DIM)
    new_v = v32.reshape(b, 1, s, N_KV_HEADS, HEAD_DIM)
    return (outs[0], outs[1], new_k, new_v)
```

```python
import functools
import math

import jax
import jax.numpy as jnp
import numpy as np
from jax import lax
from jax.experimental import pallas as pl
from jax.experimental.pallas import tpu as pltpu

F32 = jnp.float32
BF16 = jnp.bfloat16

D_MODEL = 1024
HEAD_DIM = 64
N_HEADS = 8
N_KV_HEADS = 2
ATTN_W = N_HEADS * HEAD_DIM
KV_W = N_KV_HEADS * HEAD_DIM
D_CONV = 512
IN_W = ATTN_W + 2 * KV_W + 2 * D_CONV
CONV_WIDTH = 31
CONV_PAD = CONV_WIDTH // 2
GRID_W = 64
ROPE_THETA = 10000.0
N_EXPERTS = 16
CAPACITY_FACTOR = 2
D_FF = 1024
EPS = 1e-6
LOG2_E = 1.4426950408889634

LANES = 128
SUBLANES = 8
MXU_DIM = 256
VMEM_LIMIT = 56 * 1024 * 1024

TOKEN_TILE = 1024
Q_TILE = 256
CORE_BLOCK = 1024
ROUTE_BLOCK = 1024
MIXIN_SUB = 256
MIXOUT_SUB = 256
CONV_ROWS = 128
LN_ROWS = 64
PAD_ROWS = 16


def _cparams(n_axes, flags=None):
    return pltpu.CompilerParams(
        dimension_semantics=("arbitrary",) * n_axes, vmem_limit_bytes=VMEM_LIMIT, flags=flags)


def _sigmoid(x):
    return jax.nn.sigmoid(x)


def _nt_dot(a, b):
    return lax.dot_general(a, b, (((1,), (1,)), ((), ())), preferred_element_type=F32)


def _split_bf16(x):
    hi = x.astype(BF16)
    lo = (x - hi.astype(F32)).astype(BF16)
    return hi, lo


def _mod_kernel(cond_ref, w_ref, b_ref, o_ref):
    cnd = cond_ref[...]
    s = (cnd * _sigmoid(cnd)).astype(BF16)
    o_ref[...] = jnp.dot(s, w_ref[...].astype(BF16), preferred_element_type=F32) + b_ref[...]


def _modulation(cond8, w_mod, b_mod):
    n_out = w_mod.shape[1]
    blk = 1024
    return pl.pallas_call(
        _mod_kernel,
        grid=(n_out // blk,),
        in_specs=[pl.BlockSpec((8, D_MODEL), lambda j: (0, 0)),
                  pl.BlockSpec((D_MODEL, blk), lambda j: (0, j)),
                  pl.BlockSpec((1, blk), lambda j: (0, j))],
        out_specs=pl.BlockSpec((8, blk), lambda j: (0, j)),
        out_shape=jax.ShapeDtypeStruct((8, n_out), F32),
        compiler_params=_cparams(1),
        name="modulation",
    )(cond8, w_mod, b_mod.reshape(1, n_out))


def _head_norm(t, gain, bd_ref):
    w = t.shape[1]
    cw = min(w, MXU_DIM)
    t2 = t * t
    hi, lo = _split_bf16(t2)
    bd = bd_ref[:cw, :cw]
    parts = []
    for c in range(0, w, cw):
        parts.append(jnp.dot(hi[:, c:c + cw], bd, preferred_element_type=F32)
                     + jnp.dot(lo[:, c:c + cw], bd, preferred_element_type=F32))
    ss = parts[0] if len(parts) == 1 else jnp.concatenate(parts, axis=1)
    return t * lax.rsqrt(ss * (1.0 / HEAD_DIM) + EPS) * gain


def _rope(t, cos, sin_signed):
    rows = t.shape[0]
    lane = lax.broadcasted_iota(jnp.int32, (rows, LANES), 1)
    first_half = (lane & 31) < 16
    outs = []
    for c in range(0, t.shape[1], LANES):
        tc = t[:, c:c + LANES]
        fwd = pltpu.roll(tc, LANES - 16, 1)
        bwd = pltpu.roll(tc, 16, 1)
        partner = jnp.where(first_half, fwd, bwd)
        outs.append(tc * cos + partner * sin_signed)
    return outs[0] if len(outs) == 1 else jnp.concatenate(outs, axis=1)


def _attn_layouts(k, k_sw, v, v_sw):
    lane = lax.broadcasted_iota(jnp.int32, k.shape, 1)
    low = lane < HEAD_DIM
    one_hi = jnp.where(lane == HEAD_DIM, 1.0, 0.0)
    one_lo = jnp.where(lane == 0, 1.0, 0.0)
    kd = [jnp.where(low, k, k_sw).astype(BF16), jnp.where(low, k_sw, k).astype(BF16)]
    vd = [jnp.where(low, v, one_hi).astype(BF16), jnp.where(low, one_lo, v_sw).astype(BF16),
          jnp.where(low, v_sw, one_hi).astype(BF16), jnp.where(low, one_lo, v).astype(BF16)]
    return kd, vd


def _mixin_kernel(*refs, latent):
    if latent:
        (x_ref, mod_ref, g_ref, w_ref, qg_ref, kg_ref, bd_ref, cos_ref, sin_ref,
         q_o, kd_o, vd_o, u_o, w_bf) = refs
    else:
        (x_ref, mod_ref, g_ref, w_ref, qg_ref, kg_ref, bd_ref,
         q_o, kd_o, vd_o, u_o, k32_o, v32_o, w_bf) = refs

    @pl.when(pl.program_id(0) == 0)
    def _():
        w_bf[...] = w_ref[...].astype(BF16)

    m = mod_ref[0]
    sh1 = m[:, 0:D_MODEL]
    sc1 = m[:, D_MODEL:2 * D_MODEL]
    tm = x_ref.shape[0]
    for s in range(tm // MIXIN_SUB):
        rs = slice(s * MIXIN_SUB, (s + 1) * MIXIN_SUB)
        x = x_ref[rs, :]
        ms = jnp.mean(x * x, axis=-1, keepdims=True)
        h = (x * lax.rsqrt(ms + EPS)) * g_ref[...]
        h = h * (1.0 + sc1) + sh1
        p = jnp.dot(h.astype(BF16), w_bf[...], preferred_element_type=F32)
        q = _head_norm(p[:, :ATTN_W], qg_ref[...], bd_ref)
        k = _head_norm(p[:, ATTN_W:ATTN_W + KV_W], kg_ref[...], bd_ref)
        v = p[:, ATTN_W + KV_W:ATTN_W + 2 * KV_W]
        a = p[:, ATTN_W + 2 * KV_W:ATTN_W + 2 * KV_W + D_CONV]
        gt = p[:, ATTN_W + 2 * KV_W + D_CONV:]
        if latent:
            cos = cos_ref[rs, :]
            sin = sin_ref[rs, :]
            q = _rope(q, cos, sin)
            k_att = _rope(k, cos, sin)
        else:
            k32_o[rs, :] = k
            v32_o[rs, :] = v
            k_att = k
        q_o[rs, :] = (q * (LOG2_E / math.sqrt(HEAD_DIM))).astype(BF16)
        k_sw = pltpu.roll(k_att, HEAD_DIM, 1)
        v_sw = pltpu.roll(v, HEAD_DIM, 1)
        kd, vd = _attn_layouts(k_att, k_sw, v, v_sw)
        for i, t in enumerate(kd):
            kd_o[rs, i * LANES:(i + 1) * LANES] = t
        for i, t in enumerate(vd):
            vd_o[rs, i * LANES:(i + 1) * LANES] = t
        u_o[rs, :] = a * _sigmoid(gt)


def _mix_in(x2d, mod3, g_pre, w_in_bf, qg, kg, bd, rope_tabs, *, n_seq, mod_row0, latent):
    t = x2d.shape[0]
    tm = TOKEN_TILE
    tiles_per_seq = max(n_seq // tm, 1)
    if latent:
        mod_idx = lambda i: (mod_row0 + i // tiles_per_seq, 0, 0)
    else:
        mod_idx = lambda i: (mod_row0, 0, 0)
    in_specs = [pl.BlockSpec((tm, D_MODEL), lambda i: (i, 0)),
                pl.BlockSpec((1, 1, mod3.shape[2]), mod_idx),
                pl.BlockSpec((1, D_MODEL), lambda i: (0, 0)),
                pl.BlockSpec((D_MODEL, IN_W), lambda i: (0, 0)),
                pl.BlockSpec((1, ATTN_W), lambda i: (0, 0)),
                pl.BlockSpec((1, KV_W), lambda i: (0, 0)),
                pl.BlockSpec((MXU_DIM, MXU_DIM), lambda i: (0, 0))]
    args = [x2d, mod3, g_pre, w_in_bf, qg, kg, bd]
    out_shape = [jax.ShapeDtypeStruct((t, ATTN_W), BF16),
                 jax.ShapeDtypeStruct((t, 2 * LANES), BF16),
                 jax.ShapeDtypeStruct((t, 4 * LANES), BF16),
                 jax.ShapeDtypeStruct((t, D_CONV), F32)]
    out_specs = [pl.BlockSpec((tm, ATTN_W), lambda i: (i, 0)),
                 pl.BlockSpec((tm, 2 * LANES), lambda i: (i, 0)),
                 pl.BlockSpec((tm, 4 * LANES), lambda i: (i, 0)),
                 pl.BlockSpec((tm, D_CONV), lambda i: (i, 0))]
    if latent:
        cos, sin = rope_tabs
        in_specs += [pl.BlockSpec((tm, LANES), lambda i: (i % tiles_per_seq, 0)),
                     pl.BlockSpec((tm, LANES), lambda i: (i % tiles_per_seq, 0))]
        args += [cos, sin]
    else:
        out_shape += [jax.ShapeDtypeStruct((t, KV_W), F32), jax.ShapeDtypeStruct((t, KV_W), F32)]
        out_specs += [pl.BlockSpec((tm, KV_W), lambda i: (i, 0)),
                      pl.BlockSpec((tm, KV_W), lambda i: (i, 0))]
    return pl.pallas_call(
        functools.partial(_mixin_kernel, latent=latent),
        grid=(t // tm,),
        in_specs=in_specs, out_specs=out_specs, out_shape=out_shape,
        scratch_shapes=[pltpu.VMEM((D_MODEL, IN_W), BF16)],
        compiler_params=_cparams(1),
        name="mix_in_lat" if latent else "mix_in_ctx",
    )(*args)


def _core_kernel(*refs, n_seq, latent):
    if latent:
        (q_ref, kd_ref, vd_ref, u_ref, ck_ref, cv_ref, w_ref, cb_ref, lg_ref, lb_ref,
         attn_o, conv_o, pad_ref, y_ref) = refs
    else:
        (q_ref, kd_ref, vd_ref, u_ref, w_ref, cb_ref, lg_ref, lb_ref,
         attn_o, conv_o, pad_ref, y_ref) = refs
    blk = q_ref.shape[0]
    tq = Q_TILE
    seqs = blk // n_seq
    seg = n_seq + 2 * PAD_ROWS
    zeros = jnp.zeros((PAD_ROWS, D_CONV), F32)
    for s in range(seqs):
        pad_ref[s * seg:s * seg + PAD_ROWS, :] = zeros
        pad_ref[s * seg + PAD_ROWS + n_seq:(s + 1) * seg, :] = zeros
        pad_ref[s * seg + PAD_ROWS:s * seg + PAD_ROWS + n_seq, :] = u_ref[s * n_seq:(s + 1) * n_seq, :]

    low1 = lax.broadcasted_iota(jnp.int32, (1, LANES), 1) < HEAD_DIM
    low_q = lax.broadcasted_iota(jnp.int32, (tq, LANES), 1) < HEAD_DIM
    if latent:
        ck = ck_ref[0]
        cv = cv_ref[0]
        ckd, cvd = _attn_layouts(ck, pltpu.roll(ck, HEAD_DIM, 1), cv, pltpu.roll(cv, HEAD_DIM, 1))
    base = PAD_ROWS - CONV_PAD
    ext = CONV_ROWS + SUBLANES

    def attn_pair(r0, keys, j):
        g = (2 * j) // (N_HEADS // N_KV_HEADS)
        qp = q_ref[pl.ds(r0, tq), j * LANES:(j + 1) * LANES]
        kg = kd_ref[keys, g * LANES:(g + 1) * LANES]
        outs = []
        for half in range(2):
            sel = low1 if half == 0 else jnp.logical_not(low1)
            qm = jnp.where(sel, qp, jnp.zeros_like(qp))
            s1 = _nt_dot(qm, kg)
            mx = jnp.max(s1, axis=-1, keepdims=True)
            if latent:
                s2 = _nt_dot(qm, ckd[g])
                mx = jnp.maximum(mx, jnp.max(s2, axis=-1, keepdims=True))
            vv = vd_ref[keys, (2 * g + half) * LANES:(2 * g + half + 1) * LANES]
            o = jnp.dot(jnp.exp2(s1 - mx).astype(BF16), vv, preferred_element_type=F32)
            if latent:
                o = o + jnp.dot(jnp.exp2(s2 - mx).astype(BF16), cvd[2 * g + half],
                                preferred_element_type=F32)
            d0 = HEAD_DIM * (1 - half)
            outs.append(o * (1.0 / o[:, d0:d0 + 1]))
        attn_o[pl.ds(r0, tq), j * LANES:(j + 1) * LANES] = jnp.where(low_q, outs[0], outs[1]).astype(BF16)

    def conv_tile(p0, c, j):
        ls = slice(j * LANES, (j + 1) * LANES)
        acc = jnp.zeros((CONV_ROWS, LANES), F32) + cb_ref[:, ls]
        for b in range(SUBLANES):
            z = None
            for a in range((CONV_WIDTH + base + SUBLANES - 1) // SUBLANES):
                k = SUBLANES * a + b - base
                if 0 <= k < CONV_WIDTH:
                    rows = pl.ds(p0 + c * CONV_ROWS + SUBLANES * a, ext)
                    term = pad_ref[rows, ls] * w_ref[k:k + 1, ls]
                    z = term if z is None else z + term
            if b:
                z = pltpu.roll(z, ext - b, 0)
            acc = acc + z[0:CONV_ROWS, :]
        y_ref[c * CONV_ROWS:(c + 1) * CONV_ROWS, ls] = acc

    def norm_rows(r0, c):
        acc = y_ref[c * LN_ROWS:(c + 1) * LN_ROWS, :]
        mu = jnp.mean(acc, axis=-1, keepdims=True)
        cen = acc - mu
        var = jnp.mean(cen * cen, axis=-1, keepdims=True)
        y = cen * lax.rsqrt(var + EPS) * lg_ref[...] + lb_ref[...]
        conv_o[pl.ds(r0 + c * LN_ROWS, LN_ROWS), :] = (y * _sigmoid(y)).astype(BF16)

    def body(i, carry):
        r0 = pl.multiple_of(i * tq, tq)
        if latent:
            keys = slice(None)
            p0 = r0
        else:
            keys = pl.ds(r0, n_seq)
            p0 = pl.multiple_of(i * seg, SUBLANES)
        for j in range(N_HEADS // 2):
            attn_pair(r0, keys, j)
            for c in range(tq // CONV_ROWS):
                conv_tile(p0, c, j)
        for c in range(tq // LN_ROWS):
            norm_rows(r0, c)
        return carry

    lax.fori_loop(0, blk // tq, body, 0)


def _core(q, kd, vd, u, cache, conv_w, conv_b, ln_g, ln_b, *, n_seq, latent):
    t = q.shape[0]
    blk = max(n_seq, CORE_BLOCK)
    assert n_seq in (Q_TILE, blk), "a block is one sequence, or whole sequences of one query tile each"
    row = lambda w: pl.BlockSpec((blk, w), lambda b: (b, 0))
    vec = pl.BlockSpec((1, D_CONV), lambda b: (0, 0))
    in_specs = [row(ATTN_W), row(2 * LANES), row(4 * LANES), row(D_CONV)]
    args = [q, kd, vd, u]
    if latent:
        ck, cv = cache
        in_specs += [pl.BlockSpec((1,) + ck.shape[1:], lambda b: (b, 0, 0)),
                     pl.BlockSpec((1,) + cv.shape[1:], lambda b: (b, 0, 0))]
        args += [ck, cv]
    in_specs += [pl.BlockSpec((CONV_WIDTH, D_CONV), lambda b: (0, 0)), vec, vec, vec]
    args += [conv_w, conv_b, ln_g, ln_b]
    pad_rows = (blk // n_seq) * (n_seq + 2 * PAD_ROWS)
    return pl.pallas_call(
        functools.partial(_core_kernel, n_seq=n_seq, latent=latent),
        grid=(t // blk,),
        in_specs=in_specs,
        out_specs=[row(ATTN_W), row(D_CONV)],
        out_shape=[jax.ShapeDtypeStruct((t, ATTN_W), BF16), jax.ShapeDtypeStruct((t, D_CONV), BF16)],
        scratch_shapes=[pltpu.VMEM((pad_rows, D_CONV), F32),
                        pltpu.VMEM((Q_TILE, D_CONV), F32)],
        compiler_params=_cparams(1),
        name="core_lat" if latent else "core_ctx",
    )(*args)


def _mixout_kernel(attn_ref, conv_ref, x_ref, mod_ref, wo_ref, gpost_ref, gpre_ref, wr_ref,
                   x1_o, h2_o, aff_o, wo_bf, wr_hi, wr_lo, *, n_seq):
    tm = x_ref.shape[0]

    @pl.when(pl.program_id(0) == 0)
    def _():
        wo_bf[...] = wo_ref[...].astype(BF16)
        hi, lo = _split_bf16(wr_ref[...])
        wr_hi[...] = hi
        wr_lo[...] = lo

    m = mod_ref[0]
    ga1 = m[:, 2 * D_MODEL:3 * D_MODEL]
    sh2 = m[:, 3 * D_MODEL:4 * D_MODEL]
    sc2 = m[:, 4 * D_MODEL:5 * D_MODEL]
    sub = min(MIXOUT_SUB, n_seq)
    for s in range(tm // sub):
        rs = slice(s * sub, (s + 1) * sub)
        out = (jnp.dot(attn_ref[rs, :], wo_bf[0:ATTN_W, :], preferred_element_type=F32)
               + jnp.dot(conv_ref[rs, :], wo_bf[ATTN_W:, :], preferred_element_type=F32))
        ms = jnp.mean(out * out, axis=-1, keepdims=True)
        x1 = x_ref[rs, :] + ga1 * ((out * lax.rsqrt(ms + EPS)) * gpost_ref[...])
        x1_o[rs, :] = x1
        ms2 = jnp.mean(x1 * x1, axis=-1, keepdims=True)
        h2 = ((x1 * lax.rsqrt(ms2 + EPS)) * gpre_ref[...]) * (1.0 + sc2) + sh2
        h2_o[rs, :] = h2.astype(BF16)
        h_hi, h_lo = _split_bf16(h2)
        logits = (jnp.dot(h_hi, wr_hi[...], preferred_element_type=F32)
                  + jnp.dot(h_hi, wr_lo[...], preferred_element_type=F32)
                  + jnp.dot(h_lo, wr_hi[...], preferred_element_type=F32))
        lt = logits.T[0:N_EXPERTS, :]
        mx = jnp.max(lt, axis=0, keepdims=True)
        e = jnp.exp(lt - mx)
        aff = e / jnp.sum(e, axis=0, keepdims=True)
        t0 = s * sub
        aff_o[t0 // n_seq, :, t0 % n_seq:t0 % n_seq + sub] = aff


def _mix_out(attn, conv, x2d, mod3, w_out, g_post, g_pre_ffn, wr_pad, *, n_seq, mod_row0, latent):
    t = x2d.shape[0]
    tm = TOKEN_TILE
    n_b = t // n_seq
    tiles_per_seq = max(n_seq // tm, 1)
    seqs_per_tile = max(tm // n_seq, 1)
    if latent:
        mod_idx = lambda i: (mod_row0 + i // tiles_per_seq, 0, 0)
    else:
        mod_idx = lambda i: (mod_row0, 0, 0)
    aff_w = min(tm, n_seq)
    vec = pl.BlockSpec((1, D_MODEL), lambda i: (0, 0))
    return pl.pallas_call(
        functools.partial(_mixout_kernel, n_seq=n_seq),
        grid=(t // tm,),
        in_specs=[pl.BlockSpec((tm, ATTN_W), lambda i: (i, 0)),
                  pl.BlockSpec((tm, D_CONV), lambda i: (i, 0)),
                  pl.BlockSpec((tm, D_MODEL), lambda i: (i, 0)),
                  pl.BlockSpec((1, 1, mod3.shape[2]), mod_idx),
                  pl.BlockSpec((D_MODEL, D_MODEL), lambda i: (0, 0)),
                  vec, vec,
                  pl.BlockSpec((D_MODEL, LANES), lambda i: (0, 0))],
        out_specs=[pl.BlockSpec((tm, D_MODEL), lambda i: (i, 0)),
                   pl.BlockSpec((tm, D_MODEL), lambda i: (i, 0)),
                   pl.BlockSpec((seqs_per_tile, N_EXPERTS, aff_w),
                                lambda i: (i // tiles_per_seq, 0, i % tiles_per_seq))],
        out_shape=[jax.ShapeDtypeStruct((t, D_MODEL), F32),
                   jax.ShapeDtypeStruct((t, D_MODEL), BF16),
                   jax.ShapeDtypeStruct((n_b, N_EXPERTS, n_seq), F32)],
        scratch_shapes=[pltpu.VMEM((D_MODEL, D_MODEL), BF16),
                        pltpu.VMEM((D_MODEL, LANES), BF16),
                        pltpu.VMEM((D_MODEL, LANES), BF16)],
        compiler_params=_cparams(1),
        name="mix_out_lat" if latent else "mix_out_ctx",
    )(attn, conv, x2d, mod3, w_out, g_post, g_pre_ffn, wr_pad)


def _route_kernel(aff_ref, slot_o, slot_t_o, *, n_seq, cap):
    n_b = aff_ref.shape[0]
    rows = n_b * N_EXPERTS
    a = aff_ref[...].reshape(rows, n_seq)
    capf = float(cap)

    def count_ge(thr_bits):
        thr = lax.bitcast_convert_type(thr_bits, F32)
        return jnp.sum(jnp.where(a >= thr, 1.0, 0.0), axis=-1, keepdims=True)

    def bis(i, lo):
        cand = lo | jnp.left_shift(jnp.int32(1), 29 - i)
        return jnp.where(count_ge(cand) >= capf, cand, lo)

    lo = lax.fori_loop(0, 30, bis, jnp.zeros((rows, 1), jnp.int32))
    kth = lax.bitcast_convert_type(lo, F32)
    above = lax.bitcast_convert_type(lo + 1, F32)
    gt = a >= above
    eq = jnp.logical_and(a >= kth, jnp.logical_not(gt))
    need = capf - jnp.sum(jnp.where(gt, 1.0, 0.0), axis=-1, keepdims=True)
    ri = lax.broadcasted_iota(jnp.int32, (n_seq, n_seq), 0)
    ci = lax.broadcasted_iota(jnp.int32, (n_seq, n_seq), 1)
    upper = jnp.where(ri <= ci, 1.0, 0.0).astype(BF16)
    eq_f = jnp.where(eq, 1.0, 0.0)
    eq_rank = jnp.dot(eq_f.astype(BF16), upper, preferred_element_type=F32)
    sel_f = jnp.where(gt, 1.0, jnp.where(eq_rank <= need, eq_f, 0.0))
    pos = jnp.dot(sel_f.astype(BF16), upper, preferred_element_type=F32) - 1.0
    slot = jnp.where(sel_f > 0.5, pos, -1.0)
    slot_o[...] = slot.reshape(n_b, N_EXPERTS, n_seq)
    fill = jnp.full((LANES - N_EXPERTS, n_seq), -1.0, F32)
    for b in range(n_b):
        sb = jnp.concatenate([slot[b * N_EXPERTS:(b + 1) * N_EXPERTS, :], fill], axis=0)
        slot_t_o[b] = sb.T


def _route(aff, *, n_seq, cap, name):
    n_b = aff.shape[0]
    return pl.pallas_call(
        functools.partial(_route_kernel, n_seq=n_seq, cap=cap),
        out_shape=[jax.ShapeDtypeStruct((n_b, N_EXPERTS, n_seq), F32),
                   jax.ShapeDtypeStruct((n_b, n_seq, LANES), F32)],
        compiler_params=pltpu.CompilerParams(vmem_limit_bytes=VMEM_LIMIT),
        name=name,
    )(aff)


def _gather_kernel(slot_ref, aff_ref, h_ref, xs_o, gate_o, p_ref, *, n_seq, cap):
    srow = lax.broadcasted_iota(jnp.int32, (cap, n_seq), 0).astype(F32)
    n_slots = N_EXPERTS * cap
    for s in range(slot_ref.shape[0]):
        slot = slot_ref[s]
        aff = aff_ref[s]
        for e in range(N_EXPERTS):
            hit = jnp.broadcast_to(slot[e:e + 1, :], (cap, n_seq)) == srow
            p_ref[s * n_slots + e * cap:s * n_slots + (e + 1) * cap, :] = jnp.where(hit, 1.0, 0.0).astype(BF16)
            ga = jnp.where(hit, jnp.broadcast_to(aff[e:e + 1, :], (cap, n_seq)), 0.0)
            gate_o[e, s * cap:(s + 1) * cap, :] = jnp.sum(ga, axis=-1, keepdims=True)
        xs = jnp.dot(p_ref[s * n_slots:(s + 1) * n_slots, :], h_ref[s * n_seq:(s + 1) * n_seq, :],
                     preferred_element_type=F32).astype(BF16)
        for e in range(N_EXPERTS):
            xs_o[e, s * cap:(s + 1) * cap, :] = xs[e * cap:(e + 1) * cap, :]


def _seqs_per_step(n_b, n_seq):
    return max(1, min(n_b, ROUTE_BLOCK // n_seq))


def _gather(slot, aff, h2, *, n_seq, cap, name):
    n_b = slot.shape[0]
    g = _seqs_per_step(n_b, n_seq)
    return pl.pallas_call(
        functools.partial(_gather_kernel, n_seq=n_seq, cap=cap),
        grid=(n_b // g,),
        in_specs=[pl.BlockSpec((g, N_EXPERTS, n_seq), lambda b: (b, 0, 0)),
                  pl.BlockSpec((g, N_EXPERTS, n_seq), lambda b: (b, 0, 0)),
                  pl.BlockSpec((g * n_seq, D_MODEL), lambda b: (b, 0))],
        out_specs=[pl.BlockSpec((N_EXPERTS, g * cap, D_MODEL), lambda b: (0, b, 0)),
                   pl.BlockSpec((N_EXPERTS, g * cap, 1), lambda b: (0, b, 0))],
        out_shape=[jax.ShapeDtypeStruct((N_EXPERTS, n_b * cap, D_MODEL), BF16),
                   jax.ShapeDtypeStruct((N_EXPERTS, n_b * cap, 1), F32)],
        scratch_shapes=[pltpu.VMEM((g * N_EXPERTS * cap, n_seq), BF16)],
        compiler_params=_cparams(1),
        name=name,
    )(slot, aff, h2)


FF_CHUNK = 256


def _ffn_kernel(xa_ref, xb_ref, ga_ref, gb_ref, wg_ref, wu_ref, wd_ref, ya_o, yb_o):
    for x_ref, g_ref, y_o in ((xa_ref, ga_ref, ya_o), (xb_ref, gb_ref, yb_o)):
        x = x_ref[0]
        y = jnp.zeros((x.shape[0], D_MODEL), F32)
        for c in range(0, D_FF, FF_CHUNK):
            wg = wg_ref[0, 0, :, c:c + FF_CHUNK].astype(BF16)
            wu = wu_ref[0, 0, :, c:c + FF_CHUNK].astype(BF16)
            wd = wd_ref[0, 0, c:c + FF_CHUNK, :].astype(BF16)
            hg = jnp.dot(x, wg, preferred_element_type=F32)
            hu = jnp.dot(x, wu, preferred_element_type=F32)
            act = ((hg * _sigmoid(hg)) * hu).astype(BF16)
            y = y + jnp.dot(act, wd, preferred_element_type=F32)
        y_o[0] = (y * g_ref[0]).astype(BF16)


def _ffn(xs_a, xs_b, gate_a, gate_b, w_gate, w_up, w_down):
    ra, rb = xs_a.shape[1], xs_b.shape[1]
    wspec = lambda r, c: pl.BlockSpec((1, 1, r, c), lambda e: (0, e, 0, 0))
    return pl.pallas_call(
        _ffn_kernel,
        grid=(N_EXPERTS,),
        in_specs=[pl.BlockSpec((1, ra, D_MODEL), lambda e: (e, 0, 0)),
                  pl.BlockSpec((1, rb, D_MODEL), lambda e: (e, 0, 0)),
                  pl.BlockSpec((1, ra, 1), lambda e: (e, 0, 0)),
                  pl.BlockSpec((1, rb, 1), lambda e: (e, 0, 0)),
                  wspec(D_MODEL, D_FF), wspec(D_MODEL, D_FF), wspec(D_FF, D_MODEL)],
        out_specs=[pl.BlockSpec((1, ra, D_MODEL), lambda e: (e, 0, 0)),
                   pl.BlockSpec((1, rb, D_MODEL), lambda e: (e, 0, 0))],
        out_shape=[jax.ShapeDtypeStruct((N_EXPERTS, ra, D_MODEL), BF16),
                   jax.ShapeDtypeStruct((N_EXPERTS, rb, D_MODEL), BF16)],
        compiler_params=_cparams(1),
        name="expert_ffn",
    )(xs_a, xs_b, gate_a, gate_b, w_gate, w_up, w_down)


def _combine_kernel(slot_t_ref, ys_ref, x1_ref, mod_ref, g_ref, o_ref, *, n_seq, cap):
    n_slots = N_EXPERTS * cap
    er = lax.broadcasted_iota(jnp.int32, (LANES, n_slots), 0)
    ec = lax.broadcasted_iota(jnp.int32, (LANES, n_slots), 1)
    expand = jnp.where(er == (ec >> (cap.bit_length() - 1)), 1.0, 0.0).astype(BF16)
    col_slot = (lax.broadcasted_iota(jnp.int32, (n_seq, n_slots), 1) & (cap - 1)).astype(F32)
    ga2 = mod_ref[0][:, 5 * D_MODEL:6 * D_MODEL]
    for s in range(slot_t_ref.shape[0]):
        rs = slice(s * n_seq, (s + 1) * n_seq)
        st = slot_t_ref[s].astype(BF16)
        slot_x = jnp.dot(st, expand, preferred_element_type=F32)
        pt = jnp.where(slot_x == col_slot, 1.0, 0.0).astype(BF16)
        ys = ys_ref[:, s * cap:(s + 1) * cap, :].reshape(n_slots, D_MODEL)
        moe = jnp.dot(pt, ys, preferred_element_type=F32)
        ms = jnp.mean(moe * moe, axis=-1, keepdims=True)
        o_ref[rs, :] = x1_ref[rs, :] + ga2 * ((moe * lax.rsqrt(ms + EPS)) * g_ref[...])


def _combine(slot_t, ys, x1, mod3, g_post_ffn, *, n_seq, cap, mod_row0, latent, name):
    n_b = slot_t.shape[0]
    g = _seqs_per_step(n_b, n_seq)
    if latent:
        assert g == 1, "one modulation row per grid step"
        mod_idx = lambda b: (mod_row0 + b, 0, 0)
    else:
        mod_idx = lambda b: (mod_row0, 0, 0)
    return pl.pallas_call(
        functools.partial(_combine_kernel, n_seq=n_seq, cap=cap),
        grid=(n_b // g,),
        in_specs=[pl.BlockSpec((g, n_seq, LANES), lambda b: (b, 0, 0)),
                  pl.BlockSpec((N_EXPERTS, g * cap, D_MODEL), lambda b: (0, b, 0)),
                  pl.BlockSpec((g * n_seq, D_MODEL), lambda b: (b, 0)),
                  pl.BlockSpec((1, 1, mod3.shape[2]), mod_idx),
                  pl.BlockSpec((1, D_MODEL), lambda b: (0, 0))],
        out_specs=pl.BlockSpec((g * n_seq, D_MODEL), lambda b: (b, 0)),
        out_shape=jax.ShapeDtypeStruct((n_b * n_seq, D_MODEL), F32),
        compiler_params=_cparams(1),
        name=name,
    )(slot_t, ys, x1, mod3, g_post_ffn)


def _rope_tables(n_seq):
    half = HEAD_DIM // 4
    freqs = ROPE_THETA ** (-np.arange(half, dtype=np.float64) / half)
    lane = np.arange(LANES)
    d = lane % HEAD_DIM
    j = d % half
    tok = np.arange(n_seq)
    pos_row = (tok // GRID_W).astype(np.float64)
    pos_col = (tok % GRID_W).astype(np.float64)
    pos = np.where((d < HEAD_DIM // 2)[None, :], pos_row[:, None], pos_col[:, None])
    ang = pos * freqs[j][None, :]
    sign = np.where((d % (2 * half)) < half, -1.0, 1.0)
    return (jnp.asarray(np.cos(ang), dtype=F32), jnp.asarray(np.sin(ang) * sign[None, :], dtype=F32))


def _path(x, mod3, mod_row0, latent, cache, wts):
    (g_pre_mix, g_post_mix, g_pre_ffn, g_post_ffn, w_in, qg, kg, bd, conv_w, conv_b, ln_g, ln_b,
     w_out, wr_pad) = wts
    n_b, n_seq, _ = x.shape
    x2d = x.reshape(n_b * n_seq, D_MODEL)
    rope_tabs = _rope_tables(n_seq) if latent else None
    outs = _mix_in(x2d, mod3, g_pre_mix, w_in, qg, kg, bd, rope_tabs,
                   n_seq=n_seq, mod_row0=mod_row0, latent=latent)
    q, kd, vd, u = outs[:4]
    tag = "lat" if latent else "ctx"
    attn, conv = _core(q, kd, vd, u, cache, conv_w, conv_b, ln_g, ln_b, n_seq=n_seq, latent=latent)
    x1, h2, aff = _mix_out(attn, conv, x2d, mod3, w_out, g_post_mix, g_pre_ffn, wr_pad,
                           n_seq=n_seq, mod_row0=mod_row0, latent=latent)
    cap = CAPACITY_FACTOR * n_seq // N_EXPERTS
    slot, slot_t = _route(aff, n_seq=n_seq, cap=cap, name="route_" + tag)
    xs, gate = _gather(slot, aff, h2, n_seq=n_seq, cap=cap, name="gather_" + tag)
    return dict(x1=x1, slot_t=slot_t, xs=xs, gate=gate, cap=cap, n_seq=n_seq, n_b=n_b,
                kv=outs[4:], mod_row0=mod_row0, latent=latent, tag=tag)


def kernel(x_prompt, x_sample, c, c_ctx, cache_k, cache_v, w_mod, b_mod, g_pre_mix, g_post_mix,
           g_pre_ffn, g_post_ffn, w_in, q_norm, k_norm, conv_w, conv_b, conv_ln_g, conv_ln_b,
           w_out, w_router, w_gate, w_up, w_down):
    l = 0
    n_lat = x_sample.shape[0]
    cond8 = jnp.concatenate([c_ctx[None, :], c, jnp.zeros((8 - 1 - n_lat, D_MODEL), F32)], axis=0)
    mod = _modulation(cond8, w_mod[l], b_mod[l])
    mod3 = mod.reshape(8, 1, mod.shape[1])

    bd_r = jnp.arange(MXU_DIM) // HEAD_DIM
    bd = (bd_r[:, None] == bd_r[None, :]).astype(BF16)
    row = lambda v: v.reshape(1, -1)
    wr_pad = jnp.pad(w_router[l], ((0, 0), (0, LANES - N_EXPERTS)))
    wts = (row(g_pre_mix[l]), row(g_post_mix[l]), row(g_pre_ffn[l]), row(g_post_ffn[l]),
           w_in[l], row(jnp.tile(q_norm[l], N_HEADS)), row(jnp.tile(k_norm[l], N_KV_HEADS)),
           bd, conv_w[l], row(conv_b[l]), row(conv_ln_g[l]), row(conv_ln_b[l]),
           w_out[l], wr_pad)

    past = cache_k.shape[2]
    cache = (cache_k[:, l].reshape(n_lat, past, KV_W), cache_v[:, l].reshape(n_lat, past, KV_W))
    ctx = _path(x_prompt, mod3, 0, False, None, wts)
    lat = _path(x_sample, mod3, 1, True, cache, wts)

    ys_ctx, ys_lat = _ffn(ctx["xs"], lat["xs"], ctx["gate"], lat["gate"], w_gate, w_up, w_down)

    outs = []
    for p, ys, x in ((ctx, ys_ctx, x_prompt), (lat, ys_lat, x_sample)):
        y = _combine(p["slot_t"], ys, p["x1"], mod3, row(g_post_ffn[l]), n_seq=p["n_seq"], cap=p["cap"],
                     mod_row0=p["mod_row0"], latent=p["latent"], name="combine_" + p["tag"])
        outs.append(y.reshape(x.shape))
    k32, v32 = ctx["kv"]
    b, s = x_prompt.shape[0], x_prompt.shape[1]
    new_k = k32.reshape(b, 1, s, N_KV_HEADS, HEAD_DIM)
    new_v = v32.reshape(b, 1, s, N_KV_HEADS, HEAD_DIM)
    return (outs[0], outs[1], new_k, new_v)
```
